```python
import math
import jax, jax.numpy as jnp
from jax import lax
import numpy as np

D_MODEL = 2048
BATCH = 1
SEQ = 8192
DEPTH = 1

N_META = 16
ATTN_WIDTH = D_MODEL // 2
CONV_WIDTH = D_MODEL - ATTN_WIDTH
HEAD_DIM = 128
N_ATTN_HEADS = ATTN_WIDTH // HEAD_DIM
CONV_K = 3
D_FF = -(-(8 * D_MODEL) // (3 * 256)) * 256
Q_BLOCK = 128
IN_COLS = 3 * ATTN_WIDTH + 3 * CONV_WIDTH
EPS = 1e-6

kernel_name = "hymba_stickbreak_shortconv_swiglu"


def _rmsnorm(x, g):
    xf = x.astype(jnp.float32)
    y = xf * lax.rsqrt(jnp.mean(xf * xf, axis=-1, keepdims=True) + EPS)
    return (y * g.astype(jnp.float32)).astype(x.dtype)


def _stick_breaking_block(q_blk, k_ctx, v_ctx, q_start):
    nq = q_blk.shape[1]
    nk = k_ctx.shape[1]
    z = jnp.einsum('bqhd,bkhd->bhqk', q_blk.astype(jnp.float32),
                   k_ctx.astype(jnp.float32)) / math.sqrt(HEAD_DIM)
    qpos = q_start + jnp.arange(nq)
    kpos = jnp.arange(nk)
    causal = kpos[None, :] < qpos[:, None]
    log_beta = jax.nn.log_sigmoid(z)
    log_keep = jnp.where(causal, jax.nn.log_sigmoid(-z), 0.0)
    log_pass = lax.cumsum(log_keep, axis=3, reverse=True) - log_keep
    w = jnp.where(causal, jnp.exp(log_beta + log_pass), 0.0)
    o = jnp.einsum('bhqk,bkhd->bqhd', w, v_ctx.astype(jnp.float32))
    return o.astype(q_blk.dtype)


def _stick_breaking_attention(q, k, v):
    bounds = [(0, N_META)] + [(N_META + i * Q_BLOCK, N_META + (i + 1) * Q_BLOCK)
                              for i in range(SEQ // Q_BLOCK)]
    outs = [_stick_breaking_block(q[:, s:e], k[:, :e], v[:, :e], s) for s, e in bounds]
    return jnp.concatenate(outs, axis=1)


def _causal_dwconv(u, w):
    rhs = w[:, None, :].astype(u.dtype)
    return lax.conv_general_dilated(
        u, rhs, window_strides=(1,), padding=[(CONV_K - 1, 0)],
        dimension_numbers=('NWC', 'WIO', 'NWC'), feature_group_count=u.shape[-1])


def setup_inputs(seed: int = 0) -> dict:
    key = jax.random.key(seed)
    ks = jax.random.split(key, 16)
    f32 = jnp.float32

    def gain(k, shape):
        return 1.0 + 0.02 * jax.random.normal(k, shape, f32)

    return {
        "x": jax.random.normal(ks[0], (BATCH, SEQ, D_MODEL), f32),
        "meta_tokens": jax.random.normal(ks[1], (N_META, D_MODEL), f32),
        "g_mix": gain(ks[2], (DEPTH, D_MODEL)),
        "w_in": jax.random.normal(ks[3], (DEPTH, D_MODEL, IN_COLS), f32) * D_MODEL ** -0.5,
        "g_q": gain(ks[4], (DEPTH, HEAD_DIM)),
        "g_k": gain(ks[5], (DEPTH, HEAD_DIM)),
        "conv_w": jax.random.normal(ks[6], (DEPTH, CONV_K, CONV_WIDTH), f32) * CONV_K ** -0.5,
        "g_attn_out": gain(ks[7], (DEPTH, ATTN_WIDTH)),
        "g_conv_out": gain(ks[8], (DEPTH, CONV_WIDTH)),
        "w_out": jax.random.normal(ks[9], (DEPTH, D_MODEL, D_MODEL), f32) * D_MODEL ** -0.5,
        "g_ffn": gain(ks[10], (DEPTH, D_MODEL)),
        "w_gate": jax.random.normal(ks[11], (DEPTH, D_MODEL, D_FF), f32) * D_MODEL ** -0.5,
        "w_up": jax.random.normal(ks[12], (DEPTH, D_MODEL, D_FF), f32) * D_MODEL ** -0.5,
        "w_down": jax.random.normal(ks[13], (DEPTH, D_FF, D_MODEL), f32) * D_FF ** -0.5,
    }


def reference(x, meta_tokens, g_mix, w_in, g_q, g_k, conv_w, g_attn_out, g_conv_out,
              w_out, g_ffn, w_gate, w_up, w_down):
    b = x.shape[0]
    meta = jnp.broadcast_to(meta_tokens.astype(x.dtype)[None], (b, N_META, D_MODEL))
    h = jnp.concatenate([meta, x], axis=1)
    L = h.shape[1]

    for l in range(DEPTH):
        n = _rmsnorm(h, g_mix[l])
        p = n @ w_in[l]
        q, k, v, gb, gc, u = jnp.split(
            p, np.cumsum([ATTN_WIDTH, ATTN_WIDTH, ATTN_WIDTH, CONV_WIDTH, CONV_WIDTH]), axis=-1)

        q = _rmsnorm(q.reshape(b, L, N_ATTN_HEADS, HEAD_DIM), g_q[l])
        k = _rmsnorm(k.reshape(b, L, N_ATTN_HEADS, HEAD_DIM), g_k[l])
        v = v.reshape(b, L, N_ATTN_HEADS, HEAD_DIM)
        o_attn = _stick_breaking_attention(q, k, v).reshape(b, L, ATTN_WIDTH)

        o_conv = gb * _causal_dwconv(gc * u, conv_w[l])

        o = jnp.concatenate([_rmsnorm(o_attn, g_attn_out[l]),
                             _rmsnorm(o_conv, g_conv_out[l])], axis=-1)
        h = h + o @ w_out[l]

        n2 = _rmsnorm(h, g_ffn[l])
        h = h + (jax.nn.silu(n2 @ w_gate[l]) * (n2 @ w_up[l])) @ w_down[l]

    return h[:, N_META:]
```

```python
import functools
import math

import jax
import jax.numpy as jnp
from jax import lax
from jax.experimental import pallas as pl
from jax.experimental.pallas import tpu as pltpu

D_MODEL = 2048
N_META = 16
ATTN_WIDTH = 1024
CONV_WIDTH = 1024
HEAD_DIM = 128
N_HEADS = ATTN_WIDTH // HEAD_DIM
CONV_K = 3
IN_COLS = 3 * ATTN_WIDTH + 3 * CONV_WIDTH
EPS = 1e-6

F32 = jnp.float32
BF16 = jnp.bfloat16

_SEG_Q, _SEG_K, _SEG_V, _SEG_GB, _SEG_GC, _SEG_U = range(6)

_MIB = 1024 * 1024


def _rms(x, g):
    return x * lax.rsqrt(jnp.mean(x * x, axis=-1, keepdims=True) + EPS) * g


def _in_proj_kernel(x_ref, g_ref, w_ref, gq_ref, gk_ref, p_ref, xn_ref):
    j = pl.program_id(1)

    @pl.when(j == 0)
    def _():
        xn_ref[...] = _rms(x_ref[...], g_ref[...]).astype(BF16)

    acc = jnp.dot(xn_ref[...], w_ref[...], preferred_element_type=F32)

    def head_norm(gain):
        for h in range(N_HEADS):
            sl = slice(h * HEAD_DIM, (h + 1) * HEAD_DIM)
            p_ref[:, sl] = _rms(acc[:, sl], gain).astype(BF16)

    @pl.when(j == _SEG_Q)
    def _():
        head_norm(gq_ref[...] * (1.0 / math.sqrt(HEAD_DIM)))

    @pl.when(j == _SEG_K)
    def _():
        head_norm(gk_ref[...])

    @pl.when(j > _SEG_K)
    def _():
        p_ref[...] = acc.astype(BF16)


def _in_proj(x, g_mix, w_bf16, g_q, g_k, *, tm):
    m = x.shape[0]
    tn = ATTN_WIDTH
    return pl.pallas_call(
        _in_proj_kernel,
        grid=(m // tm, IN_COLS // tn),
        in_specs=[
            pl.BlockSpec((tm, D_MODEL), lambda i, j: (i, 0)),
            pl.BlockSpec((1, D_MODEL), lambda i, j: (0, 0)),
            pl.BlockSpec((D_MODEL, tn), lambda i, j: (0, j)),
            pl.BlockSpec((1, HEAD_DIM), lambda i, j: (0, 0)),
            pl.BlockSpec((1, HEAD_DIM), lambda i, j: (0, 0)),
        ],
        out_specs=pl.BlockSpec((tm, tn), lambda i, j: (i, j)),
        out_shape=jax.ShapeDtypeStruct((m, IN_COLS), BF16),
        scratch_shapes=[pltpu.VMEM((tm, D_MODEL), BF16)],
        compiler_params=pltpu.CompilerParams(
            dimension_semantics=("arbitrary", "arbitrary"),
            vmem_limit_bytes=48 * _MIB),
        name="in_proj",
    )(x, g_mix, w_bf16, g_q, g_k)


def _split3(x):
    hi = x.astype(BF16)
    r = x - hi.astype(F32)
    mid = r.astype(BF16)
    lo = (r - mid.astype(F32)).astype(BF16)
    return hi, mid, lo


def _attn_block(q, kb, vb, tri, mask, acc, carry):
    s = lax.dot_general(q, kb, (((1,), (1,)), ((), ())), preferred_element_type=F32)
    sp = jnp.maximum(s, 0.0) + jnp.log1p(jnp.exp(-jnp.abs(s)))
    lb = s - sp
    if mask is not None:
        sp = jnp.where(mask, sp, 0.0)
    hi, mid, lo = _split3(sp)
    cum = (jnp.dot(hi, tri, preferred_element_type=F32)
           + jnp.dot(mid, tri, preferred_element_type=F32)
           + jnp.dot(lo, tri, preferred_element_type=F32))
    w = jnp.exp(lb - cum - carry)
    if mask is not None:
        w = jnp.where(mask, w, 0.0)
    acc = acc + jnp.dot(w.astype(BF16), vb, preferred_element_type=F32)
    carry = carry + jnp.sum(sp, axis=1, keepdims=True)
    return acc, carry


def _attn_kernel(q_ref, k_ref, v_ref, km_ref, vm_ref, o_ref, *, tq):
    i = pl.program_id(1)
    q = q_ref[...]
    row = lax.broadcasted_iota(jnp.int32, (tq, tq), 0)
    col = lax.broadcasted_iota(jnp.int32, (tq, tq), 1)
    tri = (row > col).astype(BF16)
    acc = jnp.zeros((tq, HEAD_DIM), F32)
    carry = jnp.zeros((tq, 1), F32)

    base = pl.multiple_of(i * tq, tq)
    acc, carry = _attn_block(q, k_ref[pl.ds(base, tq), :], v_ref[pl.ds(base, tq), :],
                             tri, col < row, acc, carry)

    def body(jj, st):
        off = pl.multiple_of((i - 1 - jj) * tq, tq)
        return _attn_block(q, k_ref[pl.ds(off, tq), :], v_ref[pl.ds(off, tq), :],
                           tri, None, *st)

    acc, carry = lax.fori_loop(0, i, body, (acc, carry))

    tm_ = km_ref.shape[0]
    mcol = lax.broadcasted_iota(jnp.int32, (tq, tm_), 1)
    acc, carry = _attn_block(q, km_ref[...], vm_ref[...], tri[:tm_, :tm_], mcol < N_META,
                             acc, carry)
    o_ref[...] = acc


def _attention(p_seq, p_meta_pad, *, tq):
    seq = p_seq.shape[0]
    mpad = p_meta_pad.shape[0]
    kcol = _SEG_K * N_HEADS
    vcol = _SEG_V * N_HEADS
    return pl.pallas_call(
        functools.partial(_attn_kernel, tq=tq),
        grid=(N_HEADS, seq // tq),
        in_specs=[
            pl.BlockSpec((tq, HEAD_DIM), lambda h, i: (i, h)),
            pl.BlockSpec((seq, HEAD_DIM), lambda h, i: (0, kcol + h)),
            pl.BlockSpec((seq, HEAD_DIM), lambda h, i: (0, vcol + h)),
            pl.BlockSpec((mpad, HEAD_DIM), lambda h, i: (0, kcol + h)),
            pl.BlockSpec((mpad, HEAD_DIM), lambda h, i: (0, vcol + h)),
        ],
        out_specs=pl.BlockSpec((tq, HEAD_DIM), lambda h, i: (i, h)),
        out_shape=jax.ShapeDtypeStruct((seq, ATTN_WIDTH), F32),
        compiler_params=pltpu.CompilerParams(
            dimension_semantics=("arbitrary", "arbitrary"),
            vmem_limit_bytes=40 * _MIB),
        name="attn",
    )(p_seq, p_seq, p_seq, p_meta_pad, p_meta_pad)


_HALO = 16


def _out_proj_kernel(oa_ref, gb_ref, gc_ref, u_ref, gch_ref, uh_ref, gcm_ref, um_ref,
                     x_ref, cw_ref, ga_ref, gcv_ref, w_ref, h_ref):
    i = pl.program_id(0)
    tm = oa_ref.shape[0]
    an = _rms(oa_ref[...], ga_ref[...]).astype(BF16)

    cu = gc_ref[...].astype(F32) * u_ref[...].astype(F32)
    halo_seq = gch_ref[...].astype(F32) * uh_ref[...].astype(F32)
    halo_meta = gcm_ref[...].astype(F32) * um_ref[...].astype(F32)
    halo = jnp.where(i == 0, halo_meta, halo_seq)
    ext = jnp.concatenate([halo, cu], axis=0)
    c1 = ext[_HALO - 1:_HALO - 1 + tm]
    c2 = ext[_HALO - 2:_HALO - 2 + tm]
    cw = cw_ref[...]
    y = cw[0:1] * c2 + cw[1:2] * c1 + cw[2:3] * cu
    oc = gb_ref[...].astype(F32) * y
    cn = _rms(oc, gcv_ref[...]).astype(BF16)

    w = w_ref[...]
    h_ref[...] = (x_ref[...]
                  + jnp.dot(an, w[:ATTN_WIDTH], preferred_element_type=F32)
                  + jnp.dot(cn, w[ATTN_WIDTH:], preferred_element_type=F32))


def _out_proj(o_attn, p_seq, p_meta, x, conv_w, g_attn_out, g_conv_out, w_bf16, *, tm):
    seq = x.shape[0]
    nb = ATTN_WIDTH // CONV_WIDTH
    hb = tm // _HALO
    seg = lambda s: (lambda i: (i, s * nb))
    halo = lambda s: (lambda i: (jnp.maximum(i * hb - 1, 0), s * nb))
    meta = lambda s: (lambda i: (0, s * nb))
    const = lambda i: (0, 0)
    return pl.pallas_call(
        _out_proj_kernel,
        grid=(seq // tm,),
        in_specs=[
            pl.BlockSpec((tm, ATTN_WIDTH), lambda i: (i, 0)),
            pl.BlockSpec((tm, CONV_WIDTH), seg(_SEG_GB)),
            pl.BlockSpec((tm, CONV_WIDTH), seg(_SEG_GC)),
            pl.BlockSpec((tm, CONV_WIDTH), seg(_SEG_U)),
            pl.BlockSpec((_HALO, CONV_WIDTH), halo(_SEG_GC)),
            pl.BlockSpec((_HALO, CONV_WIDTH), halo(_SEG_U)),
            pl.BlockSpec((N_META, CONV_WIDTH), meta(_SEG_GC)),
            pl.BlockSpec((N_META, CONV_WIDTH), meta(_SEG_U)),
            pl.BlockSpec((tm, D_MODEL), lambda i: (i, 0)),
            pl.BlockSpec((CONV_K, CONV_WIDTH), const),
            pl.BlockSpec((1, ATTN_WIDTH), const),
            pl.BlockSpec((1, CONV_WIDTH), const),
            pl.BlockSpec((D_MODEL, D_MODEL), const),
        ],
        out_specs=pl.BlockSpec((tm, D_MODEL), lambda i: (i, 0)),
        out_shape=jax.ShapeDtypeStruct((seq, D_MODEL), F32),
        compiler_params=pltpu.CompilerParams(
            dimension_semantics=("arbitrary",),
            vmem_limit_bytes=56 * _MIB),
        name="out_proj",
    )(o_attn, p_seq, p_seq, p_seq, p_seq, p_seq, p_meta, p_meta,
      x, conv_w, g_attn_out, g_conv_out, w_bf16)


def _ffn_kernel(h_ref, g_ref, wg_ref, wu_ref, wd_ref, o_ref, n_ref):
    f = pl.program_id(1)

    @pl.when(f == 0)
    def _():
        h = h_ref[...]
        n_ref[...] = _rms(h, g_ref[...]).astype(BF16)
        o_ref[...] = h

    n = n_ref[...]
    gate = jnp.dot(n, wg_ref[...], preferred_element_type=F32)
    up = jnp.dot(n, wu_ref[...], preferred_element_type=F32)
    act = (gate * (1.0 / (1.0 + jnp.exp(-gate))) * up).astype(BF16)
    o_ref[...] += jnp.dot(act, wd_ref[...], preferred_element_type=F32)


def _ffn(h1, g_ffn, wg, wu, wd, *, tm, tf):
    seq = h1.shape[0]
    d_ff = wg.shape[1]
    return pl.pallas_call(
        _ffn_kernel,
        grid=(seq // tm, d_ff // tf),
        in_specs=[
            pl.BlockSpec((tm, D_MODEL), lambda i, f: (i, 0)),
            pl.BlockSpec((1, D_MODEL), lambda i, f: (0, 0)),
            pl.BlockSpec((D_MODEL, tf), lambda i, f: (0, f)),
            pl.BlockSpec((D_MODEL, tf), lambda i, f: (0, f)),
            pl.BlockSpec((tf, D_MODEL), lambda i, f: (f, 0)),
        ],
        out_specs=pl.BlockSpec((tm, D_MODEL), lambda i, f: (i, 0)),
        out_shape=jax.ShapeDtypeStruct((seq, D_MODEL), F32),
        scratch_shapes=[pltpu.VMEM((tm, D_MODEL), BF16)],
        compiler_params=pltpu.CompilerParams(
            dimension_semantics=("arbitrary", "arbitrary"),
            vmem_limit_bytes=56 * _MIB),
        name="ffn",
    )(h1, g_ffn, wg, wu, wd)


def kernel(x, meta_tokens, g_mix, w_in, g_q, g_k, conv_w, g_attn_out, g_conv_out,
           w_out, g_ffn, w_gate, w_up, w_down):
    batch, seq, _ = x.shape
    depth = w_in.shape[0]
    assert batch == 1 and depth == 1
    xs = x[0]

    w_in_b = w_in[0].astype(BF16)
    w_out_b = w_out[0].astype(BF16)
    w_gate_b = w_gate[0].astype(BF16)
    w_up_b = w_up[0].astype(BF16)
    w_down_b = w_down[0].astype(BF16)

    p_seq = _in_proj(xs, g_mix, w_in_b, g_q, g_k, tm=1024)
    p_meta = _in_proj(meta_tokens, g_mix, w_in_b, g_q, g_k, tm=N_META)
    p_meta_pad = jnp.pad(p_meta, ((0, HEAD_DIM - N_META), (0, 0)))

    o_attn = _attention(p_seq, p_meta_pad, tq=256)
    h1 = _out_proj(o_attn, p_seq, p_meta, xs, conv_w[0], g_attn_out, g_conv_out, w_out_b,
                   tm=512)
    out = _ffn(h1, g_ffn, w_gate_b, w_up_b, w_down_b, tm=512, tf=512)
    return out[None]
```

```python
import functools
import math

import jax
import jax.numpy as jnp
from jax import lax
from jax.experimental import pallas as pl
from jax.experimental.pallas import tpu as pltpu

D_MODEL = 2048
N_META = 16
ATTN_WIDTH = 1024
CONV_WIDTH = 1024
HEAD_DIM = 128
N_HEADS = ATTN_WIDTH // HEAD_DIM
CONV_K = 3
EPS = 1e-6

F32 = jnp.float32
BF16 = jnp.bfloat16

_N_SEG = 6
_N_ATTN_SEG = 3
_SEG_Q, _SEG_K, _SEG_V = 0, 1, 2
_CSEG_GB, _CSEG_GC, _CSEG_U = 0, 1, 2

_MIB = 1024 * 1024

_SKIP = 110.0


def _rms(x, g):
    return x * lax.rsqrt(jnp.mean(x * x, axis=-1, keepdims=True) + EPS) * g


def _in_proj_kernel(x_ref, g_ref, w_ref, gq_ref, gk_ref, qkv_ref, pc_ref, xn_ref):
    j = pl.program_id(1)

    @pl.when(j == 0)
    def _():
        xn_ref[...] = _rms(x_ref[...], g_ref[...]).astype(BF16)

    acc = jnp.dot(xn_ref[...], w_ref[...], preferred_element_type=F32)

    def heads_out(fn):
        for h in range(N_HEADS):
            qkv_ref[h] = fn(acc[:, h * HEAD_DIM:(h + 1) * HEAD_DIM]).astype(BF16)

    @pl.when(j == _SEG_Q)
    def _():
        gain = gq_ref[...] * (1.0 / math.sqrt(HEAD_DIM))
        heads_out(lambda a: _rms(a, gain))

    @pl.when(j == _SEG_K)
    def _():
        gain = gk_ref[...]
        heads_out(lambda a: _rms(a, gain))

    @pl.when(j == _SEG_V)
    def _():
        heads_out(lambda a: a)

    @pl.when(j >= _N_ATTN_SEG)
    def _():
        pc_ref[...] = acc.astype(BF16)


def _in_proj(x, g_mix, w_bf16, g_q, g_k, *, tm):
    m = x.shape[0]
    tn = ATTN_WIDTH
    return pl.pallas_call(
        _in_proj_kernel,
        grid=(m // tm, _N_SEG),
        in_specs=[
            pl.BlockSpec((tm, D_MODEL), lambda i, j: (i, 0)),
            pl.BlockSpec((1, D_MODEL), lambda i, j: (0, 0)),
            pl.BlockSpec((D_MODEL, tn), lambda i, j: (0, j)),
            pl.BlockSpec((1, HEAD_DIM), lambda i, j: (0, 0)),
            pl.BlockSpec((1, HEAD_DIM), lambda i, j: (0, 0)),
        ],
        out_specs=[
            pl.BlockSpec((N_HEADS, tm, HEAD_DIM),
                         lambda i, j: (jnp.minimum(j, _N_ATTN_SEG - 1), i, 0)),
            pl.BlockSpec((tm, CONV_WIDTH),
                         lambda i, j: (i, jnp.maximum(j - _N_ATTN_SEG, 0))),
        ],
        out_shape=[
            jax.ShapeDtypeStruct((_N_ATTN_SEG * N_HEADS, m, HEAD_DIM), BF16),
            jax.ShapeDtypeStruct((m, (_N_SEG - _N_ATTN_SEG) * CONV_WIDTH), BF16),
        ],
        scratch_shapes=[pltpu.VMEM((tm, D_MODEL), BF16)],
        compiler_params=pltpu.CompilerParams(
            dimension_semantics=("arbitrary", "arbitrary"),
            vmem_limit_bytes=48 * _MIB),
        name="in_proj",
    )(x, g_mix, w_bf16, g_q, g_k)


_MASKED = -1e30


def _attn_block(q, kb, vb, tri, mask, acc, carry):
    s = lax.dot_general(q, kb, (((1,), (1,)), ((), ())), preferred_element_type=F32)
    if mask is not None:
        s = jnp.where(mask, s, _MASKED)
    sp = jnp.maximum(s, 0.0) + jnp.log(1.0 + jnp.exp(-jnp.abs(s)))
    lb = s - sp
    hi = sp.astype(BF16)
    lo = (sp - hi.astype(F32)).astype(BF16)
    cum = (jnp.dot(hi, tri, preferred_element_type=F32)
           + jnp.dot(lo, tri, preferred_element_type=F32))
    w = jnp.exp(lb - cum - carry)
    acc = acc + jnp.dot(w.astype(BF16), vb, preferred_element_type=F32)
    carry = carry + jnp.sum(sp, axis=1, keepdims=True)
    return acc, carry


def _attn_kernel(q_ref, k_ref, v_ref, km_ref, vm_ref, g_ref, o_ref,
                 acc_ref, car_ref, oall_ref, *, tq, hpg):
    i = pl.program_id(0)
    row = lax.broadcasted_iota(jnp.int32, (tq, tq), 0)
    col = lax.broadcasted_iota(jnp.int32, (tq, tq), 1)
    tri = (row > col).astype(BF16)
    diag_mask = col < row
    meta_mask = col < N_META
    base = pl.multiple_of(i * tq, tq)

    def sweep(heads, off, km, vm, mask, first):
        cmin = None
        for hh, h in enumerate(heads):
            kb = k_ref[h, pl.ds(off, tq), :] if km is None else km[h]
            vb = v_ref[h, pl.ds(off, tq), :] if vm is None else vm[h]
            if first:
                acc0 = jnp.zeros((tq, HEAD_DIM), F32)
                car0 = jnp.zeros((tq, 1), F32)
            else:
                acc0, car0 = acc_ref[hh], car_ref[hh]
            acc, car = _attn_block(q_ref[h], kb, vb, tri, mask, acc0, car0)
            acc_ref[hh] = acc
            car_ref[hh] = car
            m = jnp.min(car)
            cmin = m if cmin is None else jnp.minimum(cmin, m)
        return cmin

    def group(hg, _):
        heads = [hg * hpg + hh for hh in range(hpg)]
        cmin = sweep(heads, base, None, None, diag_mask, True)

        def cond(st):
            return jnp.logical_and(st[0] < i, st[1] < _SKIP)

        def body(st):
            off = pl.multiple_of((i - 1 - st[0]) * tq, tq)
            return st[0] + 1, sweep(heads, off, None, None, None, False)

        _, cmin = lax.while_loop(cond, body, (jnp.int32(0), cmin))

        @pl.when(cmin < _SKIP)
        def _():
            sweep(heads, None, km_ref, vm_ref, meta_mask, False)

        for hh, h in enumerate(heads):
            oall_ref[h] = acc_ref[hh]
        return 0

    lax.fori_loop(0, N_HEADS // hpg, group, 0)

    ss = jnp.zeros((tq, 1), F32)
    for h in range(N_HEADS):
        o = oall_ref[h]
        ss = ss + jnp.sum(o * o, axis=1, keepdims=True)
    inv = lax.rsqrt(ss * (1.0 / ATTN_WIDTH) + EPS)
    for h in range(N_HEADS):
        sl = slice(h * HEAD_DIM, (h + 1) * HEAD_DIM)
        o_ref[:, sl] = (oall_ref[h] * inv * g_ref[:, sl]).astype(BF16)


def _attention(qkv, qkv_meta_pad, g_attn_out, *, tq, hpg):
    seq = qkv.shape[1]
    mpad = qkv_meta_pad.shape[1]
    assert mpad == tq
    resident = dict(pipeline_mode=pl.Buffered(1))
    return pl.pallas_call(
        functools.partial(_attn_kernel, tq=tq, hpg=hpg),
        grid=(seq // tq,),
        in_specs=[
            pl.BlockSpec((N_HEADS, tq, HEAD_DIM), lambda i: (_SEG_Q, i, 0)),
            pl.BlockSpec((N_HEADS, seq, HEAD_DIM), lambda i: (_SEG_K, 0, 0), **resident),
            pl.BlockSpec((N_HEADS, seq, HEAD_DIM), lambda i: (_SEG_V, 0, 0), **resident),
            pl.BlockSpec((N_HEADS, mpad, HEAD_DIM), lambda i: (_SEG_K, 0, 0), **resident),
            pl.BlockSpec((N_HEADS, mpad, HEAD_DIM), lambda i: (_SEG_V, 0, 0), **resident),
            pl.BlockSpec((1, ATTN_WIDTH), lambda i: (0, 0)),
        ],
        out_specs=pl.BlockSpec((tq, ATTN_WIDTH), lambda i: (i, 0)),
        out_shape=jax.ShapeDtypeStruct((seq, ATTN_WIDTH), BF16),
        scratch_shapes=[
            pltpu.VMEM((hpg, tq, HEAD_DIM), F32),
            pltpu.VMEM((hpg, tq, 1), F32),
            pltpu.VMEM((N_HEADS, tq, HEAD_DIM), F32),
        ],
        compiler_params=pltpu.CompilerParams(
            dimension_semantics=("arbitrary",),
            vmem_limit_bytes=48 * _MIB),
        name="attn",
    )(qkv, qkv, qkv, qkv_meta_pad, qkv_meta_pad, g_attn_out)


_HALO = 16


def _out_proj_kernel(an_ref, gb_ref, gc_ref, u_ref, gch_ref, uh_ref, gcm_ref, um_ref,
                     x_ref, cw_ref, gcv_ref, w_ref, h_ref):
    i = pl.program_id(0)
    tm = an_ref.shape[0]

    cu = gc_ref[...].astype(F32) * u_ref[...].astype(F32)
    halo_seq = gch_ref[...].astype(F32) * uh_ref[...].astype(F32)
    halo_meta = gcm_ref[...].astype(F32) * um_ref[...].astype(F32)
    halo = jnp.where(i == 0, halo_meta, halo_seq)
    ext = jnp.concatenate([halo, cu], axis=0)
    c1 = ext[_HALO - 1:_HALO - 1 + tm]
    c2 = ext[_HALO - 2:_HALO - 2 + tm]
    cw = cw_ref[...]
    y = cw[0:1] * c2 + cw[1:2] * c1 + cw[2:3] * cu
    oc = gb_ref[...].astype(F32) * y
    cn = _rms(oc, gcv_ref[...]).astype(BF16)

    w = w_ref[...]
    h_ref[...] = (x_ref[...]
                  + jnp.dot(an_ref[...], w[:ATTN_WIDTH], preferred_element_type=F32)
                  + jnp.dot(cn, w[ATTN_WIDTH:], preferred_element_type=F32))


def _out_proj(attn_n, pc, pc_meta, x, conv_w, g_conv_out, w_bf16, *, tm):
    seq = x.shape[0]
    hb = tm // _HALO
    seg = lambda s: (lambda i: (i, s))
    halo = lambda s: (lambda i: (jnp.maximum(i * hb - 1, 0), s))
    meta = lambda s: (lambda i: (0, s))
    const = lambda i: (0, 0)
    return pl.pallas_call(
        _out_proj_kernel,
        grid=(seq // tm,),
        in_specs=[
            pl.BlockSpec((tm, ATTN_WIDTH), lambda i: (i, 0)),
            pl.BlockSpec((tm, CONV_WIDTH), seg(_CSEG_GB)),
            pl.BlockSpec((tm, CONV_WIDTH), seg(_CSEG_GC)),
            pl.BlockSpec((tm, CONV_WIDTH), seg(_CSEG_U)),
            pl.BlockSpec((_HALO, CONV_WIDTH), halo(_CSEG_GC)),
            pl.BlockSpec((_HALO, CONV_WIDTH), halo(_CSEG_U)),
            pl.BlockSpec((N_META, CONV_WIDTH), meta(_CSEG_GC)),
            pl.BlockSpec((N_META, CONV_WIDTH), meta(_CSEG_U)),
            pl.BlockSpec((tm, D_MODEL), lambda i: (i, 0)),
            pl.BlockSpec((CONV_K, CONV_WIDTH), const),
            pl.BlockSpec((1, CONV_WIDTH), const),
            pl.BlockSpec((D_MODEL, D_MODEL), const),
        ],
        out_specs=pl.BlockSpec((tm, D_MODEL), lambda i: (i, 0)),
        out_shape=jax.ShapeDtypeStruct((seq, D_MODEL), F32),
        compiler_params=pltpu.CompilerParams(
            dimension_semantics=("arbitrary",),
            vmem_limit_bytes=56 * _MIB),
        name="out_proj",
    )(attn_n, pc, pc, pc, pc, pc, pc_meta, pc_meta, x, conv_w, g_conv_out, w_bf16)


def _ffn_kernel(h_ref, g_ref, wg_ref, wu_ref, wd_ref, o_ref, n_ref):
    f = pl.program_id(1)

    @pl.when(f == 0)
    def _():
        h = h_ref[...]
        n_ref[...] = _rms(h, g_ref[...]).astype(BF16)
        o_ref[...] = h

    n = n_ref[...]
    gate = jnp.dot(n, wg_ref[...], preferred_element_type=F32)
    up = jnp.dot(n, wu_ref[...], preferred_element_type=F32)
    act = (gate * (1.0 / (1.0 + jnp.exp(-gate))) * up).astype(BF16)
    o_ref[...] += jnp.dot(act, wd_ref[...], preferred_element_type=F32)


def _ffn(h1, g_ffn, wg, wu, wd, *, tm, tf):
    seq = h1.shape[0]
    d_ff = wg.shape[1]
    return pl.pallas_call(
        _ffn_kernel,
        grid=(seq // tm, d_ff // tf),
        in_specs=[
            pl.BlockSpec((tm, D_MODEL), lambda i, f: (i, 0)),
            pl.BlockSpec((1, D_MODEL), lambda i, f: (0, 0)),
            pl.BlockSpec((D_MODEL, tf), lambda i, f: (0, f)),
            pl.BlockSpec((D_MODEL, tf), lambda i, f: (0, f)),
            pl.BlockSpec((tf, D_MODEL), lambda i, f: (f, 0)),
        ],
        out_specs=pl.BlockSpec((tm, D_MODEL), lambda i, f: (i, 0)),
        out_shape=jax.ShapeDtypeStruct((seq, D_MODEL), F32),
        scratch_shapes=[pltpu.VMEM((tm, D_MODEL), BF16)],
        compiler_params=pltpu.CompilerParams(
            dimension_semantics=("arbitrary", "arbitrary"),
            vmem_limit_bytes=56 * _MIB),
        name="ffn",
    )(h1, g_ffn, wg, wu, wd)


def kernel(x, meta_tokens, g_mix, w_in, g_q, g_k, conv_w, g_attn_out, g_conv_out,
           w_out, g_ffn, w_gate, w_up, w_down):
    batch, seq, _ = x.shape
    depth = w_in.shape[0]
    assert batch == 1 and depth == 1
    xs = x[0]
    tq = 256

    w_in_b = w_in[0].astype(BF16)
    w_out_b = w_out[0].astype(BF16)
    w_gate_b = w_gate[0].astype(BF16)
    w_up_b = w_up[0].astype(BF16)
    w_down_b = w_down[0].astype(BF16)

    qkv, pc = _in_proj(xs, g_mix, w_in_b, g_q, g_k, tm=1024)
    qkv_meta, pc_meta = _in_proj(meta_tokens, g_mix, w_in_b, g_q, g_k, tm=N_META)
    qkv_meta_pad = jnp.pad(qkv_meta, ((0, 0), (0, tq - N_META), (0, 0)))

    attn_n = _attention(qkv, qkv_meta_pad, g_attn_out, tq=tq, hpg=2)
    h1 = _out_proj(attn_n, pc, pc_meta, xs, conv_w[0], g_conv_out, w_out_b, tm=512)
    out = _ffn(h1, g_ffn, w_gate_b, w_up_b, w_down_b, tm=512, tf=512)
    return out[None]
```

```python
import functools
import math

import jax
import jax.numpy as jnp
from jax import lax
from jax.experimental import pallas as pl
from jax.experimental.pallas import tpu as pltpu

D_MODEL = 2048
N_META = 16
ATTN_WIDTH = 1024
CONV_WIDTH = 1024
HEAD_DIM = 128
N_HEADS = ATTN_WIDTH // HEAD_DIM
CONV_K = 3
EPS = 1e-6

F32 = jnp.float32
BF16 = jnp.bfloat16

_N_SEG = 6
_N_ATTN_SEG = 3
_SEG_Q, _SEG_K, _SEG_V = 0, 1, 2
_CSEG_GB, _CSEG_GC, _CSEG_U = 0, 1, 2

_MIB = 1024 * 1024

_SKIP = 110.0


def _rms(x, g):
    return x * lax.rsqrt(jnp.mean(x * x, axis=-1, keepdims=True) + EPS) * g


_N_CHUNK = 4
_CHUNK = ATTN_WIDTH // _N_CHUNK
_HEADS_PER_CHUNK = _CHUNK // HEAD_DIM
_CAST_TILE = 512


def _in_proj_kernel(*refs, cast_w, n_side):
    x_ref, g_ref, w_ref, gq_ref, gk_ref = refs[:5]
    side_in = refs[5:5 + n_side]
    outs = refs[5 + n_side:]
    qkv_ref, pc_ref = outs[:2]
    outs = outs[2:]
    if cast_w:
        wb_ref, outs = outs[0], outs[1:]
    side_out, xn_ref = outs[:n_side], outs[n_side]
    j = pl.program_id(1)

    @pl.when(j == 0)
    def _():
        xn_ref[...] = _rms(x_ref[...], g_ref[...]).astype(BF16)

    def chunks(epilogue):
        for src, dst in zip(side_in, side_out):
            dst[...] = src[...].astype(BF16)
        xn = xn_ref[...]
        for c in range(_N_CHUNK):
            cols = slice(c * _CHUNK, (c + 1) * _CHUNK)
            if cast_w:
                wb_ref[:, cols] = w_ref[:, cols].astype(BF16)
                w = wb_ref[:, cols]
            else:
                w = w_ref[:, cols]
            epilogue(c, jnp.dot(xn, w, preferred_element_type=F32))

    def heads_out(fn):
        def epilogue(c, acc):
            for h in range(_HEADS_PER_CHUNK):
                a = acc[:, h * HEAD_DIM:(h + 1) * HEAD_DIM]
                qkv_ref[c * _HEADS_PER_CHUNK + h] = fn(a).astype(BF16)
        return epilogue

    @pl.when(j <= _SEG_K)
    def _():
        gain = jnp.where(j == _SEG_Q, gq_ref[...] * (1.0 / math.sqrt(HEAD_DIM)), gk_ref[...])
        chunks(heads_out(lambda a: _rms(a, gain)))

    @pl.when(j == _SEG_V)
    def _():
        chunks(heads_out(lambda a: a))

    @pl.when(j >= _N_ATTN_SEG)
    def _():
        def epilogue(c, acc):
            pc_ref[:, c * _CHUNK:(c + 1) * _CHUNK] = acc.astype(BF16)
        chunks(epilogue)


def _in_proj(x, g_mix, w, g_q, g_k, side=(), *, tm):
    m = x.shape[0]
    tn = ATTN_WIDTH
    n_i = m // tm
    cast_w = w.dtype != BF16

    def side_spec(a):
        rt, ct = a.shape[0] // _CAST_TILE, a.shape[1] // _CAST_TILE
        assert rt * _CAST_TILE == a.shape[0] and ct * _CAST_TILE == a.shape[1]
        assert rt * ct <= n_i * _N_SEG

        def index(i, j):
            t = jnp.minimum(i * _N_SEG + j, rt * ct - 1)
            return t // ct, t % ct
        return pl.BlockSpec((_CAST_TILE, _CAST_TILE), index)

    side_specs = [side_spec(a) for a in side]
    out_specs = [
        pl.BlockSpec((N_HEADS, tm, HEAD_DIM),
                     lambda i, j: (jnp.minimum(j, _N_ATTN_SEG - 1), i, 0)),
        pl.BlockSpec((tm, CONV_WIDTH),
                     lambda i, j: (i, jnp.maximum(j - _N_ATTN_SEG, 0))),
    ]
    out_shape = [
        jax.ShapeDtypeStruct((_N_ATTN_SEG * N_HEADS, m, HEAD_DIM), BF16),
        jax.ShapeDtypeStruct((m, (_N_SEG - _N_ATTN_SEG) * CONV_WIDTH), BF16),
    ]
    if cast_w:
        assert n_i == 1
        out_specs.append(pl.BlockSpec((D_MODEL, tn), lambda i, j: (0, j)))
        out_shape.append(jax.ShapeDtypeStruct(w.shape, BF16))
    out_specs += side_specs
    out_shape += [jax.ShapeDtypeStruct(a.shape, BF16) for a in side]
    return pl.pallas_call(
        functools.partial(_in_proj_kernel, cast_w=cast_w, n_side=len(side)),
        grid=(n_i, _N_SEG),
        in_specs=[
            pl.BlockSpec((tm, D_MODEL), lambda i, j: (i, 0)),
            pl.BlockSpec((1, D_MODEL), lambda i, j: (0, 0)),
            pl.BlockSpec((D_MODEL, tn), lambda i, j: (0, j)),
            pl.BlockSpec((1, HEAD_DIM), lambda i, j: (0, 0)),
            pl.BlockSpec((1, HEAD_DIM), lambda i, j: (0, 0)),
        ] + side_specs,
        out_specs=out_specs,
        out_shape=out_shape,
        scratch_shapes=[pltpu.VMEM((tm, D_MODEL), BF16)],
        compiler_params=pltpu.CompilerParams(
            dimension_semantics=("arbitrary", "arbitrary"),
            vmem_limit_bytes=56 * _MIB),
        name="in_proj",
    )(x, g_mix, w, g_q, g_k, *side)


_MASKED = -1e30


def _attn_blocks(qs, kbs, vbs, tri, mask, accs, carries):
    n = len(qs)
    ss = [lax.dot_general(qs[c], kbs[c], (((1,), (1,)), ((), ())), preferred_element_type=F32)
          for c in range(n)]
    lbs, sps, cums = [], [], []
    for c in range(n):
        s = ss[c] if mask is None else jnp.where(mask, ss[c], _MASKED)
        sp = jnp.maximum(s, 0.0) + jnp.log(1.0 + jnp.exp(-jnp.abs(s)))
        lbs.append(s - sp)
        sps.append(sp)
        hi = sp.astype(BF16)
        lo = (sp - hi.astype(F32)).astype(BF16)
        cums.append(jnp.dot(hi, tri, preferred_element_type=F32)
                    + jnp.dot(lo, tri, preferred_element_type=F32))
    out = []
    for c in range(n):
        w = jnp.exp(lbs[c] - cums[c] - carries[c])
        acc = accs[c] + jnp.dot(w.astype(BF16), vbs[c], preferred_element_type=F32)
        out.append((acc, carries[c] + jnp.sum(sps[c], axis=1, keepdims=True)))
    return out


def _attn_kernel(q_ref, k_ref, v_ref, km_ref, vm_ref, g_ref, o_ref,
                 acc_ref, car_ref, oall_ref, *, tq, hpg):
    i = pl.program_id(0)
    row = lax.broadcasted_iota(jnp.int32, (tq, tq), 0)
    col = lax.broadcasted_iota(jnp.int32, (tq, tq), 1)
    tri = (row > col).astype(BF16)
    diag_mask = col < row
    meta_mask = col < N_META
    base = pl.multiple_of(i * tq, tq)

    def sweep(heads, off, km, vm, mask, first):
        if first:
            init = [(jnp.zeros((tq, HEAD_DIM), F32), jnp.zeros((tq, 1), F32))] * len(heads)
        else:
            init = [(acc_ref[hh], car_ref[hh]) for hh in range(len(heads))]
        kbs = [k_ref[h, pl.ds(off, tq), :] if km is None else km[h] for h in heads]
        vbs = [v_ref[h, pl.ds(off, tq), :] if vm is None else vm[h] for h in heads]
        res = _attn_blocks([q_ref[h] for h in heads], kbs, vbs, tri, mask,
                           [a for a, _ in init], [c for _, c in init])
        cmin = None
        for hh, (acc, car) in enumerate(res):
            acc_ref[hh] = acc
            car_ref[hh] = car
            m = jnp.min(car)
            cmin = m if cmin is None else jnp.minimum(cmin, m)
        return cmin

    def group(hg, _):
        heads = [hg * hpg + hh for hh in range(hpg)]
        cmin = sweep(heads, base, None, None, diag_mask, True)

        def cond(st):
            return jnp.logical_and(st[0] < i, st[1] < _SKIP)

        def body(st):
            off = pl.multiple_of((i - 1 - st[0]) * tq, tq)
            return st[0] + 1, sweep(heads, off, None, None, None, False)

        _, cmin = lax.while_loop(cond, body, (jnp.int32(0), cmin))

        @pl.when(cmin < _SKIP)
        def _():
            sweep(heads, None, km_ref, vm_ref, meta_mask, False)

        for hh, h in enumerate(heads):
            oall_ref[h] = acc_ref[hh]
        return 0

    lax.fori_loop(0, N_HEADS // hpg, group, 0)

    ss = jnp.zeros((tq, 1), F32)
    for h in range(N_HEADS):
        o = oall_ref[h]
        ss = ss + jnp.sum(o * o, axis=1, keepdims=True)
    inv = lax.rsqrt(ss * (1.0 / ATTN_WIDTH) + EPS)
    for h in range(N_HEADS):
        sl = slice(h * HEAD_DIM, (h + 1) * HEAD_DIM)
        o_ref[:, sl] = (oall_ref[h] * inv * g_ref[:, sl]).astype(BF16)


def _attention(qkv, qkv_meta_pad, g_attn_out, *, tq, hpg):
    seq = qkv.shape[1]
    mpad = qkv_meta_pad.shape[1]
    assert mpad == tq
    resident = dict(pipeline_mode=pl.Buffered(1))
    return pl.pallas_call(
        functools.partial(_attn_kernel, tq=tq, hpg=hpg),
        grid=(seq // tq,),
        in_specs=[
            pl.BlockSpec((N_HEADS, tq, HEAD_DIM), lambda i: (_SEG_Q, i, 0)),
            pl.BlockSpec((N_HEADS, seq, HEAD_DIM), lambda i: (_SEG_K, 0, 0), **resident),
            pl.BlockSpec((N_HEADS, seq, HEAD_DIM), lambda i: (_SEG_V, 0, 0), **resident),
            pl.BlockSpec((N_HEADS, mpad, HEAD_DIM), lambda i: (_SEG_K, 0, 0), **resident),
            pl.BlockSpec((N_HEADS, mpad, HEAD_DIM), lambda i: (_SEG_V, 0, 0), **resident),
            pl.BlockSpec((1, ATTN_WIDTH), lambda i: (0, 0)),
        ],
        out_specs=pl.BlockSpec((tq, ATTN_WIDTH), lambda i: (i, 0)),
        out_shape=jax.ShapeDtypeStruct((seq, ATTN_WIDTH), BF16),
        scratch_shapes=[
            pltpu.VMEM((hpg, tq, HEAD_DIM), F32),
            pltpu.VMEM((hpg, tq, 1), F32),
            pltpu.VMEM((N_HEADS, tq, HEAD_DIM), F32),
        ],
        compiler_params=pltpu.CompilerParams(
            dimension_semantics=("arbitrary",),
            vmem_limit_bytes=48 * _MIB),
        name="attn",
    )(qkv, qkv, qkv, qkv_meta_pad, qkv_meta_pad, g_attn_out)


_HALO = 16
_CONV_ROWS = 64
_CONV_COLS = 256


def _out_proj_kernel(an_ref, gb_ref, gc_ref, u_ref, gch_ref, uh_ref, gcm_ref, um_ref,
                     x_ref, cw_ref, gcv_ref, w_ref, h_ref, ocg_ref):
    i = pl.program_id(0)
    tm = an_ref.shape[0]
    h_attn = jnp.dot(an_ref[...], w_ref[:ATTN_WIDTH, :], preferred_element_type=F32)

    inv = []
    for r0 in range(0, tm, _CONV_ROWS):
        ss = jnp.zeros((_CONV_ROWS, 1), F32)
        for c0 in range(0, CONV_WIDTH, _CONV_COLS):
            cols = slice(c0, c0 + _CONV_COLS)
            if r0 == 0:
                h_seq = gch_ref[:, cols].astype(F32) * uh_ref[:, cols].astype(F32)
                h_meta = gcm_ref[:, cols].astype(F32) * um_ref[:, cols].astype(F32)
                halo = jnp.where(i == 0, h_meta, h_seq)
                cu = gc_ref[:_CONV_ROWS, cols].astype(F32) * u_ref[:_CONV_ROWS, cols].astype(F32)
                ext = jnp.concatenate([halo, cu], axis=0)
            else:
                rows = slice(r0 - _HALO, r0 + _CONV_ROWS)
                ext = gc_ref[rows, cols].astype(F32) * u_ref[rows, cols].astype(F32)
            cw = cw_ref[:, cols]
            y = (cw[0:1] * ext[_HALO - 2:_HALO - 2 + _CONV_ROWS]
                 + cw[1:2] * ext[_HALO - 1:_HALO - 1 + _CONV_ROWS]
                 + cw[2:3] * ext[_HALO:])
            oc = gb_ref[r0:r0 + _CONV_ROWS, cols].astype(F32) * y
            ss = ss + jnp.sum(oc * oc, axis=1, keepdims=True)
            ocg_ref[r0:r0 + _CONV_ROWS, cols] = (oc * gcv_ref[:, cols]).astype(BF16)
        inv.append(lax.rsqrt(ss * (1.0 / CONV_WIDTH) + EPS))
    inv = jnp.concatenate(inv, axis=0)

    h_conv = jnp.dot(ocg_ref[...], w_ref[ATTN_WIDTH:, :], preferred_element_type=F32)
    h_ref[...] = x_ref[...] + h_attn + inv * h_conv


def _out_proj(attn_n, pc, pc_meta, x, conv_w, g_conv_out, w_bf16, *, tm):
    seq = x.shape[0]
    hb = tm // _HALO
    seg = lambda s: (lambda i: (i, s))
    halo = lambda s: (lambda i: (jnp.maximum(i * hb - 1, 0), s))
    meta = lambda s: (lambda i: (0, s))
    const = lambda i: (0, 0)
    return pl.pallas_call(
        _out_proj_kernel,
        grid=(seq // tm,),
        in_specs=[
            pl.BlockSpec((tm, ATTN_WIDTH), lambda i: (i, 0)),
            pl.BlockSpec((tm, CONV_WIDTH), seg(_CSEG_GB)),
            pl.BlockSpec((tm, CONV_WIDTH), seg(_CSEG_GC)),
            pl.BlockSpec((tm, CONV_WIDTH), seg(_CSEG_U)),
            pl.BlockSpec((_HALO, CONV_WIDTH), halo(_CSEG_GC)),
            pl.BlockSpec((_HALO, CONV_WIDTH), halo(_CSEG_U)),
            pl.BlockSpec((N_META, CONV_WIDTH), meta(_CSEG_GC)),
            pl.BlockSpec((N_META, CONV_WIDTH), meta(_CSEG_U)),
            pl.BlockSpec((tm, D_MODEL), lambda i: (i, 0)),
            pl.BlockSpec((CONV_K, CONV_WIDTH), const),
            pl.BlockSpec((1, CONV_WIDTH), const),
            pl.BlockSpec((D_MODEL, D_MODEL), const),
        ],
        out_specs=pl.BlockSpec((tm, D_MODEL), lambda i: (i, 0)),
        out_shape=jax.ShapeDtypeStruct((seq, D_MODEL), F32),
        scratch_shapes=[pltpu.VMEM((tm, CONV_WIDTH), BF16)],
        compiler_params=pltpu.CompilerParams(
            dimension_semantics=("arbitrary",),
            vmem_limit_bytes=56 * _MIB),
        name="out_proj",
    )(attn_n, pc, pc, pc, pc, pc, pc_meta, pc_meta, x, conv_w, g_conv_out, w_bf16)


def _ffn_kernel(h_ref, g_ref, wg_ref, wu_ref, wd_ref, o_ref, n_ref):
    f = pl.program_id(1)

    @pl.when(f == 0)
    def _():
        h = h_ref[...]
        n_ref[...] = _rms(h, g_ref[...]).astype(BF16)
        o_ref[...] = h

    n = n_ref[...]
    gate = jnp.dot(n, wg_ref[...], preferred_element_type=F32)
    up = jnp.dot(n, wu_ref[...], preferred_element_type=F32)
    act = (gate * (1.0 / (1.0 + jnp.exp(-gate))) * up).astype(BF16)
    o_ref[...] += jnp.dot(act, wd_ref[...], preferred_element_type=F32)


def _ffn(h1, g_ffn, wg, wu, wd, *, tm, tf):
    seq = h1.shape[0]
    d_ff = wg.shape[1]
    return pl.pallas_call(
        _ffn_kernel,
        grid=(seq // tm, d_ff // tf),
        in_specs=[
            pl.BlockSpec((tm, D_MODEL), lambda i, f: (i, 0)),
            pl.BlockSpec((1, D_MODEL), lambda i, f: (0, 0)),
            pl.BlockSpec((D_MODEL, tf), lambda i, f: (0, f)),
            pl.BlockSpec((D_MODEL, tf), lambda i, f: (0, f)),
            pl.BlockSpec((tf, D_MODEL), lambda i, f: (f, 0)),
        ],
        out_specs=pl.BlockSpec((tm, D_MODEL), lambda i, f: (i, 0)),
        out_shape=jax.ShapeDtypeStruct((seq, D_MODEL), F32),
        scratch_shapes=[pltpu.VMEM((tm, D_MODEL), BF16)],
        compiler_params=pltpu.CompilerParams(
            dimension_semantics=("arbitrary", "arbitrary"),
            vmem_limit_bytes=60 * _MIB),
        name="ffn",
    )(h1, g_ffn, wg, wu, wd)


def kernel(x, meta_tokens, g_mix, w_in, g_q, g_k, conv_w, g_attn_out, g_conv_out,
           w_out, g_ffn, w_gate, w_up, w_down):
    batch, seq, _ = x.shape
    depth = w_in.shape[0]
    assert batch == 1 and depth == 1
    xs = x[0]
    tq = 256

    qkv_meta, pc_meta, w_in_b = _in_proj(meta_tokens, g_mix, w_in[0], g_q, g_k, tm=N_META)
    qkv, pc, w_gate_b, w_up_b, w_down_b, w_out_b = _in_proj(
        xs, g_mix, w_in_b, g_q, g_k, side=(w_gate[0], w_up[0], w_down[0], w_out[0]), tm=1024)
    qkv_meta_pad = jnp.pad(qkv_meta, ((0, 0), (0, tq - N_META), (0, 0)))

    attn_n = _attention(qkv, qkv_meta_pad, g_attn_out, tq=tq, hpg=4)
    h1 = _out_proj(attn_n, pc, pc_meta, xs, conv_w[0], g_conv_out, w_out_b, tm=512)
    out = _ffn(h1, g_ffn, w_gate_b, w_up_b, w_down_b, tm=1024, tf=512)
    return out[None]
```

```python
import functools
import math

import jax
import jax.numpy as jnp
from jax import lax
from jax.experimental import pallas as pl
from jax.experimental.pallas import tpu as pltpu

D_MODEL = 2048
N_META = 16
ATTN_WIDTH = 1024
CONV_WIDTH = 1024
HEAD_DIM = 128
N_HEADS = ATTN_WIDTH // HEAD_DIM
CONV_K = 3
EPS = 1e-6

F32 = jnp.float32
BF16 = jnp.bfloat16

_N_SEG = 6
_N_ATTN_SEG = 3
_SEG_Q, _SEG_K, _SEG_V = 0, 1, 2
_CSEG_GB, _CSEG_GC, _CSEG_U = 0, 1, 2

_MIB = 1024 * 1024

_SKIP = 110.0


def _rms(x, g):
    return x * lax.rsqrt(jnp.mean(x * x, axis=-1, keepdims=True) + EPS) * g


_N_CHUNK = 4
_CHUNK = ATTN_WIDTH // _N_CHUNK
_HEADS_PER_CHUNK = _CHUNK // HEAD_DIM
_CAST_STEPS = 32


def _cast_spec(a, step_of):
    rows = a.shape[0] // _CAST_STEPS
    assert rows * _CAST_STEPS == a.shape[0] and rows % 16 == 0
    return pl.BlockSpec((rows, a.shape[1]),
                        lambda *idx: (jnp.minimum(step_of(*idx), _CAST_STEPS - 1), 0))


def _cast_side(side_in, side_out):
    for src, dst in zip(side_in, side_out):
        dst[...] = src[...].astype(BF16)


def _in_proj_kernel(*refs, cast_w, n_side):
    x_ref, g_ref, w_ref, gq_ref, gk_ref = refs[:5]
    side_in = refs[5:5 + n_side]
    outs = refs[5 + n_side:]
    qkv_ref, pc_ref = outs[:2]
    outs = outs[2:]
    if cast_w:
        wb_ref, outs = outs[0], outs[1:]
    side_out, xn_ref = outs[:n_side], outs[n_side]
    j = pl.program_id(1)

    @pl.when(j == 0)
    def _():
        xn_ref[...] = _rms(x_ref[...], g_ref[...]).astype(BF16)

    def chunks(epilogue):
        _cast_side(side_in, side_out)
        xn = xn_ref[...]
        for c in range(_N_CHUNK):
            cols = slice(c * _CHUNK, (c + 1) * _CHUNK)
            if cast_w:
                wb_ref[:, cols] = w_ref[:, cols].astype(BF16)
                w = wb_ref[:, cols]
            else:
                w = w_ref[:, cols]
            epilogue(c, jnp.dot(xn, w, preferred_element_type=F32))

    def heads_out(fn):
        def epilogue(c, acc):
            for h in range(_HEADS_PER_CHUNK):
                a = acc[:, h * HEAD_DIM:(h + 1) * HEAD_DIM]
                qkv_ref[c * _HEADS_PER_CHUNK + h] = fn(a).astype(BF16)
        return epilogue

    @pl.when(j <= _SEG_K)
    def _():
        gain = jnp.where(j == _SEG_Q, gq_ref[...] * (1.0 / math.sqrt(HEAD_DIM)), gk_ref[...])
        chunks(heads_out(lambda a: _rms(a, gain)))

    @pl.when(j == _SEG_V)
    def _():
        chunks(heads_out(lambda a: a))

    @pl.when(j >= _N_ATTN_SEG)
    def _():
        def epilogue(c, acc):
            pc_ref[:, c * _CHUNK:(c + 1) * _CHUNK] = acc.astype(BF16)
        chunks(epilogue)


def _in_proj(x, g_mix, w, g_q, g_k, side=(), *, tm):
    m = x.shape[0]
    tn = ATTN_WIDTH
    n_i = m // tm
    cast_w = w.dtype != BF16

    assert not side or n_i * _N_SEG >= _CAST_STEPS
    side_specs = [_cast_spec(a, lambda i, j: i * _N_SEG + j) for a in side]
    out_specs = [
        pl.BlockSpec((N_HEADS, tm, HEAD_DIM),
                     lambda i, j: (jnp.minimum(j, _N_ATTN_SEG - 1), i, 0)),
        pl.BlockSpec((tm, CONV_WIDTH),
                     lambda i, j: (i, jnp.maximum(j - _N_ATTN_SEG, 0))),
    ]
    out_shape = [
        jax.ShapeDtypeStruct((_N_ATTN_SEG * N_HEADS, m, HEAD_DIM), BF16),
        jax.ShapeDtypeStruct((m, (_N_SEG - _N_ATTN_SEG) * CONV_WIDTH), BF16),
    ]
    if cast_w:
        assert n_i == 1
        out_specs.append(pl.BlockSpec((D_MODEL, tn), lambda i, j: (0, j)))
        out_shape.append(jax.ShapeDtypeStruct(w.shape, BF16))
    out_specs += side_specs
    out_shape += [jax.ShapeDtypeStruct(a.shape, BF16) for a in side]
    return pl.pallas_call(
        functools.partial(_in_proj_kernel, cast_w=cast_w, n_side=len(side)),
        grid=(n_i, _N_SEG),
        in_specs=[
            pl.BlockSpec((tm, D_MODEL), lambda i, j: (i, 0)),
            pl.BlockSpec((1, D_MODEL), lambda i, j: (0, 0)),
            pl.BlockSpec((D_MODEL, tn), lambda i, j: (0, j)),
            pl.BlockSpec((1, HEAD_DIM), lambda i, j: (0, 0)),
            pl.BlockSpec((1, HEAD_DIM), lambda i, j: (0, 0)),
        ] + side_specs,
        out_specs=out_specs,
        out_shape=out_shape,
        scratch_shapes=[pltpu.VMEM((tm, D_MODEL), BF16)],
        compiler_params=pltpu.CompilerParams(
            dimension_semantics=("arbitrary", "arbitrary"),
            vmem_limit_bytes=56 * _MIB),
        name="in_proj",
    )(x, g_mix, w, g_q, g_k, *side)


_MASKED = -1e30


def _attn_blocks(qs, kbs, vbs, tri, mask, accs, carries):
    n = len(qs)
    ss = [lax.dot_general(qs[c], kbs[c], (((1,), (1,)), ((), ())), preferred_element_type=F32)
          for c in range(n)]
    lbs, sps, cums = [], [], []
    for c in range(n):
        s = ss[c] if mask is None else jnp.where(mask, ss[c], _MASKED)
        sp = jnp.maximum(s, 0.0) + jnp.log(1.0 + jnp.exp(-jnp.abs(s)))
        lbs.append(s - sp)
        sps.append(sp)
        hi = sp.astype(BF16)
        lo = (sp - hi.astype(F32)).astype(BF16)
        cums.append(jnp.dot(hi, tri, preferred_element_type=F32)
                    + jnp.dot(lo, tri, preferred_element_type=F32))
    out = []
    for c in range(n):
        w = jnp.exp(lbs[c] - cums[c] - carries[c])
        acc = accs[c] + jnp.dot(w.astype(BF16), vbs[c], preferred_element_type=F32)
        out.append((acc, carries[c] + jnp.sum(sps[c], axis=1, keepdims=True)))
    return out


def _attn_kernel(*refs, tq, hpg, n_side):
    q_ref, k_ref, v_ref, km_ref, vm_ref, g_ref = refs[:6]
    side_in = refs[6:6 + n_side]
    o_ref = refs[6 + n_side]
    side_out = refs[7 + n_side:7 + 2 * n_side]
    acc_ref, car_ref, oall_ref = refs[7 + 2 * n_side:]
    _cast_side(side_in, side_out)

    i = pl.program_id(0)
    row = lax.broadcasted_iota(jnp.int32, (tq, tq), 0)
    col = lax.broadcasted_iota(jnp.int32, (tq, tq), 1)
    tri = (row > col).astype(BF16)
    diag_mask = col < row
    meta_mask = col < N_META
    base = pl.multiple_of(i * tq, tq)

    def sweep(heads, off, km, vm, mask, first):
        if first:
            init = [(jnp.zeros((tq, HEAD_DIM), F32), jnp.zeros((tq, 1), F32))] * len(heads)
        else:
            init = [(acc_ref[hh], car_ref[hh]) for hh in range(len(heads))]
        kbs = [k_ref[h, pl.ds(off, tq), :] if km is None else km[h] for h in heads]
        vbs = [v_ref[h, pl.ds(off, tq), :] if vm is None else vm[h] for h in heads]
        res = _attn_blocks([q_ref[h] for h in heads], kbs, vbs, tri, mask,
                           [a for a, _ in init], [c for _, c in init])
        cmin = None
        for hh, (acc, car) in enumerate(res):
            acc_ref[hh] = acc
            car_ref[hh] = car
            m = jnp.min(car)
            cmin = m if cmin is None else jnp.minimum(cmin, m)
        return cmin

    def group(hg, _):
        heads = [hg * hpg + hh for hh in range(hpg)]
        cmin = sweep(heads, base, None, None, diag_mask, True)

        def cond(st):
            return jnp.logical_and(st[0] < i, st[1] < _SKIP)

        def body(st):
            off = pl.multiple_of((i - 1 - st[0]) * tq, tq)
            return st[0] + 1, sweep(heads, off, None, None, None, False)

        _, cmin = lax.while_loop(cond, body, (jnp.int32(0), cmin))

        @pl.when(cmin < _SKIP)
        def _():
            sweep(heads, None, km_ref, vm_ref, meta_mask, False)

        for hh, h in enumerate(heads):
            oall_ref[h] = acc_ref[hh]
        return 0

    lax.fori_loop(0, N_HEADS // hpg, group, 0)

    ss = jnp.zeros((tq, 1), F32)
    for h in range(N_HEADS):
        o = oall_ref[h]
        ss = ss + jnp.sum(o * o, axis=1, keepdims=True)
    inv = lax.rsqrt(ss * (1.0 / ATTN_WIDTH) + EPS)
    for h in range(N_HEADS):
        sl = slice(h * HEAD_DIM, (h + 1) * HEAD_DIM)
        o_ref[:, sl] = (oall_ref[h] * inv * g_ref[:, sl]).astype(BF16)


def _attention(qkv, qkv_meta_pad, g_attn_out, side=(), *, tq, hpg):
    seq = qkv.shape[1]
    mpad = qkv_meta_pad.shape[1]
    assert mpad == tq
    assert not side or seq // tq >= _CAST_STEPS
    resident = dict(pipeline_mode=pl.Buffered(1))
    side_specs = [_cast_spec(a, lambda i: i) for a in side]
    return pl.pallas_call(
        functools.partial(_attn_kernel, tq=tq, hpg=hpg, n_side=len(side)),
        grid=(seq // tq,),
        in_specs=[
            pl.BlockSpec((N_HEADS, tq, HEAD_DIM), lambda i: (_SEG_Q, i, 0)),
            pl.BlockSpec((N_HEADS, seq, HEAD_DIM), lambda i: (_SEG_K, 0, 0), **resident),
            pl.BlockSpec((N_HEADS, seq, HEAD_DIM), lambda i: (_SEG_V, 0, 0), **resident),
            pl.BlockSpec((N_HEADS, mpad, HEAD_DIM), lambda i: (_SEG_K, 0, 0), **resident),
            pl.BlockSpec((N_HEADS, mpad, HEAD_DIM), lambda i: (_SEG_V, 0, 0), **resident),
            pl.BlockSpec((1, ATTN_WIDTH), lambda i: (0, 0)),
        ] + side_specs,
        out_specs=[pl.BlockSpec((tq, ATTN_WIDTH), lambda i: (i, 0))] + side_specs,
        out_shape=[jax.ShapeDtypeStruct((seq, ATTN_WIDTH), BF16)]
        + [jax.ShapeDtypeStruct(a.shape, BF16) for a in side],
        scratch_shapes=[
            pltpu.VMEM((hpg, tq, HEAD_DIM), F32),
            pltpu.VMEM((hpg, tq, 1), F32),
            pltpu.VMEM((N_HEADS, tq, HEAD_DIM), F32),
        ],
        compiler_params=pltpu.CompilerParams(
            dimension_semantics=("arbitrary",),
            vmem_limit_bytes=58 * _MIB),
        name="attn",
    )(qkv, qkv, qkv, qkv_meta_pad, qkv_meta_pad, g_attn_out, *side)


_HALO = 16
_CONV_ROWS = 64
_CONV_COLS = 256


def _out_proj_kernel(an_ref, gb_ref, gc_ref, u_ref, gch_ref, uh_ref, gcm_ref, um_ref,
                     x_ref, cw_ref, gcv_ref, w_ref, h_ref, ocg_ref, hc_ref):
    i = pl.program_id(0)
    tm = an_ref.shape[0]
    n_rows = tm // _CONV_ROWS
    n_cols = CONV_WIDTH // _CONV_COLS
    nw = D_MODEL // n_cols
    an = an_ref[...]
    ssq = [jnp.zeros((_CONV_ROWS, 128), F32)] * n_rows
    for c in range(n_cols):
        c0 = c * _CONV_COLS
        cols = slice(c0, c0 + _CONV_COLS)
        ocols = slice(c * nw, (c + 1) * nw)
        h_ref[:, ocols] = x_ref[:, ocols] + jnp.dot(an, w_ref[:ATTN_WIDTH, ocols],
                                                    preferred_element_type=F32)
        for r in range(n_rows):
            r0 = r * _CONV_ROWS
            if r0 == 0:
                h_seq = gch_ref[:, cols].astype(F32) * uh_ref[:, cols].astype(F32)
                h_meta = gcm_ref[:, cols].astype(F32) * um_ref[:, cols].astype(F32)
                halo = jnp.where(i == 0, h_meta, h_seq)
                cu = gc_ref[:_CONV_ROWS, cols].astype(F32) * u_ref[:_CONV_ROWS, cols].astype(F32)
                ext = jnp.concatenate([halo, cu], axis=0)
            else:
                rows = slice(r0 - _HALO, r0 + _CONV_ROWS)
                ext = gc_ref[rows, cols].astype(F32) * u_ref[rows, cols].astype(F32)
            cw = cw_ref[:, cols]
            y = (cw[0:1] * pltpu.roll(ext, 2, axis=0)[_HALO:]
                 + cw[1:2] * pltpu.roll(ext, 1, axis=0)[_HALO:]
                 + cw[2:3] * ext[_HALO:])
            oc = gb_ref[r0:r0 + _CONV_ROWS, cols].astype(F32) * y
            sq = oc * oc
            for l0 in range(0, _CONV_COLS, 128):
                ssq[r] = ssq[r] + sq[:, l0:l0 + 128]
            ocg_ref[r0:r0 + _CONV_ROWS, cols] = (oc * gcv_ref[:, cols]).astype(BF16)
        part = jnp.dot(ocg_ref[:, cols], w_ref[ATTN_WIDTH + c0:ATTN_WIDTH + c0 + _CONV_COLS, :],
                       preferred_element_type=F32)
        if c == 0:
            hc_ref[...] = part
        else:
            hc_ref[...] += part
    inv = jnp.concatenate(
        [lax.rsqrt(jnp.sum(s, axis=1, keepdims=True) * (1.0 / CONV_WIDTH) + EPS) for s in ssq],
        axis=0)
    h_ref[...] += inv * hc_ref[...]


def _out_proj(attn_n, pc, pc_meta, x, conv_w, g_conv_out, w_bf16, *, tm):
    seq = x.shape[0]
    hb = tm // _HALO
    seg = lambda s: (lambda i: (i, s))
    halo = lambda s: (lambda i: (jnp.maximum(i * hb - 1, 0), s))
    meta = lambda s: (lambda i: (0, s))
    const = lambda i: (0, 0)
    return pl.pallas_call(
        _out_proj_kernel,
        grid=(seq // tm,),
        in_specs=[
            pl.BlockSpec((tm, ATTN_WIDTH), lambda i: (i, 0)),
            pl.BlockSpec((tm, CONV_WIDTH), seg(_CSEG_GB)),
            pl.BlockSpec((tm, CONV_WIDTH), seg(_CSEG_GC)),
            pl.BlockSpec((tm, CONV_WIDTH), seg(_CSEG_U)),
            pl.BlockSpec((_HALO, CONV_WIDTH), halo(_CSEG_GC)),
            pl.BlockSpec((_HALO, CONV_WIDTH), halo(_CSEG_U)),
            pl.BlockSpec((N_META, CONV_WIDTH), meta(_CSEG_GC)),
            pl.BlockSpec((N_META, CONV_WIDTH), meta(_CSEG_U)),
            pl.BlockSpec((tm, D_MODEL), lambda i: (i, 0)),
            pl.BlockSpec((CONV_K, CONV_WIDTH), const),
            pl.BlockSpec((1, CONV_WIDTH), const),
            pl.BlockSpec((D_MODEL, D_MODEL), const),
        ],
        out_specs=pl.BlockSpec((tm, D_MODEL), lambda i: (i, 0)),
        out_shape=jax.ShapeDtypeStruct((seq, D_MODEL), F32),
        scratch_shapes=[pltpu.VMEM((tm, CONV_WIDTH), BF16), pltpu.VMEM((tm, D_MODEL), F32)],
        compiler_params=pltpu.CompilerParams(
            dimension_semantics=("arbitrary",),
            vmem_limit_bytes=56 * _MIB),
        name="out_proj",
    )(attn_n, pc, pc, pc, pc, pc, pc_meta, pc_meta, x, conv_w, g_conv_out, w_bf16)


def _ffn_kernel(h_ref, g_ref, wg_ref, wu_ref, wd_ref, o_ref, n_ref):
    f = pl.program_id(1)

    @pl.when(f == 0)
    def _():
        h = h_ref[...]
        n_ref[...] = _rms(h, g_ref[...]).astype(BF16)
        o_ref[...] = h

    n = n_ref[...]
    gate = jnp.dot(n, wg_ref[...], preferred_element_type=F32)
    up = jnp.dot(n, wu_ref[...], preferred_element_type=F32)
    act = (gate * (1.0 / (1.0 + jnp.exp(-gate))) * up).astype(BF16)
    o_ref[...] += jnp.dot(act, wd_ref[...], preferred_element_type=F32)


def _ffn(h1, g_ffn, wg, wu, wd, *, tm, tf):
    seq = h1.shape[0]
    d_ff = wg.shape[1]
    return pl.pallas_call(
        _ffn_kernel,
        grid=(seq // tm, d_ff // tf),
        in_specs=[
            pl.BlockSpec((tm, D_MODEL), lambda i, f: (i, 0)),
            pl.BlockSpec((1, D_MODEL), lambda i, f: (0, 0)),
            pl.BlockSpec((D_MODEL, tf), lambda i, f: (0, f)),
            pl.BlockSpec((D_MODEL, tf), lambda i, f: (0, f)),
            pl.BlockSpec((tf, D_MODEL), lambda i, f: (f, 0)),
        ],
        out_specs=pl.BlockSpec((tm, D_MODEL), lambda i, f: (i, 0)),
        out_shape=jax.ShapeDtypeStruct((seq, D_MODEL), F32),
        scratch_shapes=[pltpu.VMEM((tm, D_MODEL), BF16)],
        compiler_params=pltpu.CompilerParams(
            dimension_semantics=("arbitrary", "arbitrary"),
            vmem_limit_bytes=60 * _MIB),
        name="ffn",
    )(h1, g_ffn, wg, wu, wd)


def kernel(x, meta_tokens, g_mix, w_in, g_q, g_k, conv_w, g_attn_out, g_conv_out,
           w_out, g_ffn, w_gate, w_up, w_down):
    batch, seq, _ = x.shape
    depth = w_in.shape[0]
    assert batch == 1 and depth == 1
    xs = x[0]
    tq = 256

    qkv_meta, pc_meta, w_in_b = _in_proj(meta_tokens, g_mix, w_in[0], g_q, g_k, tm=N_META)
    qkv, pc, w_out_b = _in_proj(xs, g_mix, w_in_b, g_q, g_k, side=(w_out[0],), tm=1024)
    qkv_meta_pad = jnp.pad(qkv_meta, ((0, 0), (0, tq - N_META), (0, 0)))

    attn_n, w_gate_b, w_up_b, w_down_b = _attention(
        qkv, qkv_meta_pad, g_attn_out, side=(w_gate[0], w_up[0], w_down[0]), tq=tq, hpg=4)
    h1 = _out_proj(attn_n, pc, pc_meta, xs, conv_w[0], g_conv_out, w_out_b, tm=512)
    out = _ffn(h1, g_ffn, w_gate_b, w_up_b, w_down_b, tm=1024, tf=512)
    return out[None]
```

```python
import functools
import math

import jax
import jax.numpy as jnp
from jax import lax
from jax.experimental import pallas as pl
from jax.experimental.pallas import tpu as pltpu

D_MODEL = 2048
N_META = 16
ATTN_WIDTH = 1024
CONV_WIDTH = 1024
HEAD_DIM = 128
N_HEADS = ATTN_WIDTH // HEAD_DIM
CONV_K = 3
EPS = 1e-6

F32 = jnp.float32
BF16 = jnp.bfloat16

_N_SEG = 6
_N_ATTN_SEG = 3
_SEG_Q, _SEG_K, _SEG_V = 0, 1, 2
_CSEG_GB, _CSEG_GC, _CSEG_U = 0, 1, 2

_MIB = 1024 * 1024

_SKIP = 110.0


def _rms(x, g):
    return x * lax.rsqrt(jnp.mean(x * x, axis=-1, keepdims=True) + EPS) * g


_N_CHUNK = 4
_CHUNK = ATTN_WIDTH // _N_CHUNK
_HEADS_PER_CHUNK = _CHUNK // HEAD_DIM
_CAST_STEPS = 32


def _cast_specs(a, chunk, step_of):
    rows = a.shape[0] // _CAST_STEPS
    assert rows * _CAST_STEPS == a.shape[0] and rows % 16 == 0

    def step(*idx):
        return jnp.minimum(step_of(*idx), _CAST_STEPS - 1)
    in_spec = pl.BlockSpec((rows, a.shape[1]), lambda *idx: (step(*idx), 0))
    if chunk is None:
        return in_spec, in_spec, jax.ShapeDtypeStruct(a.shape, BF16)
    n = a.shape[1] // chunk
    assert n * chunk == a.shape[1]
    out_spec = pl.BlockSpec((n, rows, chunk), lambda *idx: (0, step(*idx), 0))
    return in_spec, out_spec, jax.ShapeDtypeStruct((n, a.shape[0], chunk), BF16)


def _cast_side(side_in, side_out):
    for src, dst in zip(side_in, side_out):
        if len(dst.shape) == 2:
            dst[...] = src[...].astype(BF16)
        else:
            n, _, chunk = dst.shape
            for f in range(n):
                dst[f] = src[:, f * chunk:(f + 1) * chunk].astype(BF16)


def _in_proj_kernel(*refs, cast_w, n_side):
    x_ref, g_ref, w_ref, gq_ref, gk_ref = refs[:5]
    side_in = refs[5:5 + n_side]
    outs = refs[5 + n_side:]
    qkv_ref, pc_ref = outs[:2]
    outs = outs[2:]
    if cast_w:
        wb_ref, outs = outs[0], outs[1:]
    side_out, xn_ref = outs[:n_side], outs[n_side]
    j = pl.program_id(1)

    @pl.when(j == 0)
    def _():
        xn_ref[...] = _rms(x_ref[...], g_ref[...]).astype(BF16)

    def chunks(epilogue):
        _cast_side(side_in, side_out)
        xn = xn_ref[...]
        for c in range(_N_CHUNK):
            cols = slice(c * _CHUNK, (c + 1) * _CHUNK)
            if cast_w:
                wb_ref[0, :, cols] = w_ref[:, cols].astype(BF16)
                w = wb_ref[0, :, cols]
            else:
                w = w_ref[0, :, cols]
            epilogue(c, jnp.dot(xn, w, preferred_element_type=F32))

    def heads_out(fn):
        def epilogue(c, acc):
            for h in range(_HEADS_PER_CHUNK):
                a = acc[:, h * HEAD_DIM:(h + 1) * HEAD_DIM]
                qkv_ref[c * _HEADS_PER_CHUNK + h] = fn(a).astype(BF16)
        return epilogue

    @pl.when(j <= _SEG_K)
    def _():
        gain = jnp.where(j == _SEG_Q, gq_ref[...] * (1.0 / math.sqrt(HEAD_DIM)), gk_ref[...])
        chunks(heads_out(lambda a: _rms(a, gain)))

    @pl.when(j == _SEG_V)
    def _():
        chunks(heads_out(lambda a: a))

    @pl.when(j >= _N_ATTN_SEG)
    def _():
        def epilogue(c, acc):
            pc_ref[:, c * _CHUNK:(c + 1) * _CHUNK] = acc.astype(BF16)
        chunks(epilogue)


def _in_proj(x, g_mix, w, g_q, g_k, side=(), *, tm):
    m = x.shape[0]
    tn = ATTN_WIDTH
    n_i = m // tm
    cast_w = w.dtype != BF16

    assert not side or n_i * _N_SEG >= _CAST_STEPS
    casts = [_cast_specs(a, chunk, lambda i, j: i * _N_SEG + j) for a, chunk in side]
    w_seg_spec = pl.BlockSpec((1, D_MODEL, tn), lambda i, j: (j, 0, 0))
    out_specs = [
        pl.BlockSpec((N_HEADS, tm, HEAD_DIM),
                     lambda i, j: (jnp.minimum(j, _N_ATTN_SEG - 1), i, 0)),
        pl.BlockSpec((tm, CONV_WIDTH),
                     lambda i, j: (i, jnp.maximum(j - _N_ATTN_SEG, 0))),
    ]
    out_shape = [
        jax.ShapeDtypeStruct((_N_ATTN_SEG * N_HEADS, m, HEAD_DIM), BF16),
        jax.ShapeDtypeStruct((m, (_N_SEG - _N_ATTN_SEG) * CONV_WIDTH), BF16),
    ]
    if cast_w:
        assert n_i == 1
        out_specs.append(w_seg_spec)
        out_shape.append(jax.ShapeDtypeStruct((_N_SEG, D_MODEL, tn), BF16))
        w_spec = pl.BlockSpec((D_MODEL, tn), lambda i, j: (0, j))
    else:
        w_spec = w_seg_spec
    out_specs += [c[1] for c in casts]
    out_shape += [c[2] for c in casts]
    return pl.pallas_call(
        functools.partial(_in_proj_kernel, cast_w=cast_w, n_side=len(side)),
        grid=(n_i, _N_SEG),
        in_specs=[
            pl.BlockSpec((tm, D_MODEL), lambda i, j: (i, 0)),
            pl.BlockSpec((1, D_MODEL), lambda i, j: (0, 0)),
            w_spec,
            pl.BlockSpec((1, HEAD_DIM), lambda i, j: (0, 0)),
            pl.BlockSpec((1, HEAD_DIM), lambda i, j: (0, 0)),
        ] + [c[0] for c in casts],
        out_specs=out_specs,
        out_shape=out_shape,
        scratch_shapes=[pltpu.VMEM((tm, D_MODEL), BF16)],
        compiler_params=pltpu.CompilerParams(
            dimension_semantics=("arbitrary", "arbitrary"),
            vmem_limit_bytes=56 * _MIB),
        name="in_proj",
    )(x, g_mix, w, g_q, g_k, *[a for a, _ in side])


_MASKED = -1e30


def _attn_blocks(qs, kbs, vbs, tri, mask, accs, carries):
    n = len(qs)
    ss = [lax.dot_general(qs[c], kbs[c], (((1,), (1,)), ((), ())), preferred_element_type=F32)
          for c in range(n)]
    lbs, sps, cums = [], [], []
    for c in range(n):
        s = ss[c] if mask is None else jnp.where(mask, ss[c], _MASKED)
        sp = jnp.maximum(s, 0.0) + jnp.log(1.0 + jnp.exp(-jnp.abs(s)))
        lbs.append(s - sp)
        sps.append(sp)
        hi = sp.astype(BF16)
        lo = (sp - hi.astype(F32)).astype(BF16)
        cums.append(jnp.dot(hi, tri, preferred_element_type=F32)
                    + jnp.dot(lo, tri, preferred_element_type=F32))
    out = []
    for c in range(n):
        w = jnp.exp(lbs[c] - cums[c] - carries[c])
        acc = accs[c] + jnp.dot(w.astype(BF16), vbs[c], preferred_element_type=F32)
        out.append((acc, carries[c] + jnp.sum(sps[c], axis=1, keepdims=True)))
    return out


def _attn_kernel(*refs, tq, hpg, n_side):
    q_ref, k_ref, v_ref, km_ref, vm_ref, g_ref = refs[:6]
    side_in = refs[6:6 + n_side]
    o_ref = refs[6 + n_side]
    side_out = refs[7 + n_side:7 + 2 * n_side]
    acc_ref, car_ref, oall_ref = refs[7 + 2 * n_side:]
    _cast_side(side_in, side_out)

    i = pl.program_id(0)
    row = lax.broadcasted_iota(jnp.int32, (tq, tq), 0)
    col = lax.broadcasted_iota(jnp.int32, (tq, tq), 1)
    tri = (row > col).astype(BF16)
    diag_mask = col < row
    meta_mask = col < N_META
    base = pl.multiple_of(i * tq, tq)

    def sweep(heads, off, km, vm, mask, first):
        if first:
            init = [(jnp.zeros((tq, HEAD_DIM), F32), jnp.zeros((tq, 1), F32))] * len(heads)
        else:
            init = [(acc_ref[hh], car_ref[hh]) for hh in range(len(heads))]
        kbs = [k_ref[h, pl.ds(off, tq), :] if km is None else km[h] for h in heads]
        vbs = [v_ref[h, pl.ds(off, tq), :] if vm is None else vm[h] for h in heads]
        res = _attn_blocks([q_ref[h] for h in heads], kbs, vbs, tri, mask,
                           [a for a, _ in init], [c for _, c in init])
        cmin = None
        for hh, (acc, car) in enumerate(res):
            acc_ref[hh] = acc
            car_ref[hh] = car
            m = jnp.min(car)
            cmin = m if cmin is None else jnp.minimum(cmin, m)
        return cmin

    def group(hg, _):
        heads = [hg * hpg + hh for hh in range(hpg)]
        cmin = sweep(heads, base, None, None, diag_mask, True)

        def cond(st):
            return jnp.logical_and(st[0] < i, st[1] < _SKIP)

        def body(st):
            off = pl.multiple_of((i - 1 - st[0]) * tq, tq)
            return st[0] + 1, sweep(heads, off, None, None, None, False)

        _, cmin = lax.while_loop(cond, body, (jnp.int32(0), cmin))

        @pl.when(cmin < _SKIP)
        def _():
            sweep(heads, None, km_ref, vm_ref, meta_mask, False)

        for hh, h in enumerate(heads):
            oall_ref[h] = acc_ref[hh]
        return 0

    if hpg == N_HEADS:
        group(0, 0)
    else:
        lax.fori_loop(0, N_HEADS // hpg, group, 0)

    ssq = jnp.zeros((tq, HEAD_DIM), F32)
    for h in range(N_HEADS):
        o = oall_ref[h]
        ssq = ssq + o * o
    inv = lax.rsqrt(jnp.sum(ssq, axis=1, keepdims=True) * (1.0 / ATTN_WIDTH) + EPS)
    for h in range(N_HEADS):
        sl = slice(h * HEAD_DIM, (h + 1) * HEAD_DIM)
        o_ref[:, sl] = (oall_ref[h] * inv * g_ref[:, sl]).astype(BF16)


def _attention(qkv, qkv_meta_pad, g_attn_out, side=(), *, tq, hpg):
    seq = qkv.shape[1]
    mpad = qkv_meta_pad.shape[1]
    assert mpad == tq
    assert not side or seq // tq >= _CAST_STEPS
    resident = dict(pipeline_mode=pl.Buffered(1))
    casts = [_cast_specs(a, chunk, lambda i: i) for a, chunk in side]
    return pl.pallas_call(
        functools.partial(_attn_kernel, tq=tq, hpg=hpg, n_side=len(side)),
        grid=(seq // tq,),
        in_specs=[
            pl.BlockSpec((N_HEADS, tq, HEAD_DIM), lambda i: (_SEG_Q, i, 0)),
            pl.BlockSpec((N_HEADS, seq, HEAD_DIM), lambda i: (_SEG_K, 0, 0), **resident),
            pl.BlockSpec((N_HEADS, seq, HEAD_DIM), lambda i: (_SEG_V, 0, 0), **resident),
            pl.BlockSpec((N_HEADS, mpad, HEAD_DIM), lambda i: (_SEG_K, 0, 0), **resident),
            pl.BlockSpec((N_HEADS, mpad, HEAD_DIM), lambda i: (_SEG_V, 0, 0), **resident),
            pl.BlockSpec((1, ATTN_WIDTH), lambda i: (0, 0)),
        ] + [c[0] for c in casts],
        out_specs=[pl.BlockSpec((tq, ATTN_WIDTH), lambda i: (i, 0))] + [c[1] for c in casts],
        out_shape=[jax.ShapeDtypeStruct((seq, ATTN_WIDTH), BF16)] + [c[2] for c in casts],
        scratch_shapes=[
            pltpu.VMEM((hpg, tq, HEAD_DIM), F32),
            pltpu.VMEM((hpg, tq, 1), F32),
            pltpu.VMEM((N_HEADS, tq, HEAD_DIM), F32),
        ],
        compiler_params=pltpu.CompilerParams(
            dimension_semantics=("arbitrary",),
            vmem_limit_bytes=60 * _MIB),
        name="attn",
    )(qkv, qkv, qkv, qkv_meta_pad, qkv_meta_pad, g_attn_out, *[a for a, _ in side])


_HALO = 16
_CONV_ROWS = 64
_CONV_COLS = 256


def _out_proj_kernel(an_ref, gb_ref, gc_ref, u_ref, gch_ref, uh_ref, gcm_ref, um_ref,
                     x_ref, cw_ref, gcv_ref, w_ref, h_ref, ocg_ref, hc_ref):
    i = pl.program_id(0)
    tm = an_ref.shape[0]
    n_rows = tm // _CONV_ROWS
    n_cols = CONV_WIDTH // _CONV_COLS
    nw = D_MODEL // n_cols
    an = an_ref[...]
    ssq = [jnp.zeros((_CONV_ROWS, 128), F32)] * n_rows
    for c in range(n_cols):
        c0 = c * _CONV_COLS
        cols = slice(c0, c0 + _CONV_COLS)
        ocols = slice(c * nw, (c + 1) * nw)
        h_ref[:, ocols] = x_ref[:, ocols] + jnp.dot(an, w_ref[:ATTN_WIDTH, ocols],
                                                    preferred_element_type=F32)
        for r in range(n_rows):
            r0 = r * _CONV_ROWS
            if r0 == 0:
                h_seq = gch_ref[:, cols].astype(F32) * uh_ref[:, cols].astype(F32)
                h_meta = gcm_ref[:, cols].astype(F32) * um_ref[:, cols].astype(F32)
                halo = jnp.where(i == 0, h_meta, h_seq)
                cu = gc_ref[:_CONV_ROWS, cols].astype(F32) * u_ref[:_CONV_ROWS, cols].astype(F32)
                ext = jnp.concatenate([halo, cu], axis=0)
            else:
                rows = slice(r0 - _HALO, r0 + _CONV_ROWS)
                ext = gc_ref[rows, cols].astype(F32) * u_ref[rows, cols].astype(F32)
            cw = cw_ref[:, cols]
            y = (cw[0:1] * pltpu.roll(ext, 2, axis=0)[_HALO:]
                 + cw[1:2] * pltpu.roll(ext, 1, axis=0)[_HALO:]
                 + cw[2:3] * ext[_HALO:])
            oc = gb_ref[r0:r0 + _CONV_ROWS, cols].astype(F32) * y
            sq = oc * oc
            for l0 in range(0, _CONV_COLS, 128):
                ssq[r] = ssq[r] + sq[:, l0:l0 + 128]
            ocg_ref[r0:r0 + _CONV_ROWS, cols] = (oc * gcv_ref[:, cols]).astype(BF16)
        part = jnp.dot(ocg_ref[:, cols], w_ref[ATTN_WIDTH + c0:ATTN_WIDTH + c0 + _CONV_COLS, :],
                       preferred_element_type=F32)
        if c == 0:
            hc_ref[...] = part
        else:
            hc_ref[...] += part
    inv = jnp.concatenate(
        [lax.rsqrt(jnp.sum(s, axis=1, keepdims=True) * (1.0 / CONV_WIDTH) + EPS) for s in ssq],
        axis=0)
    h_ref[...] += inv * hc_ref[...]


def _out_proj(attn_n, pc, pc_meta, x, conv_w, g_conv_out, w_bf16, *, tm):
    seq = x.shape[0]
    hb = tm // _HALO
    seg = lambda s: (lambda i: (i, s))
    halo = lambda s: (lambda i: (jnp.maximum(i * hb - 1, 0), s))
    meta = lambda s: (lambda i: (0, s))
    const = lambda i: (0, 0)
    return pl.pallas_call(
        _out_proj_kernel,
        grid=(seq // tm,),
        in_specs=[
            pl.BlockSpec((tm, ATTN_WIDTH), lambda i: (i, 0)),
            pl.BlockSpec((tm, CONV_WIDTH), seg(_CSEG_GB)),
            pl.BlockSpec((tm, CONV_WIDTH), seg(_CSEG_GC)),
            pl.BlockSpec((tm, CONV_WIDTH), seg(_CSEG_U)),
            pl.BlockSpec((_HALO, CONV_WIDTH), halo(_CSEG_GC)),
            pl.BlockSpec((_HALO, CONV_WIDTH), halo(_CSEG_U)),
            pl.BlockSpec((N_META, CONV_WIDTH), meta(_CSEG_GC)),
            pl.BlockSpec((N_META, CONV_WIDTH), meta(_CSEG_U)),
            pl.BlockSpec((tm, D_MODEL), lambda i: (i, 0)),
            pl.BlockSpec((CONV_K, CONV_WIDTH), const),
            pl.BlockSpec((1, CONV_WIDTH), const),
            pl.BlockSpec((D_MODEL, D_MODEL), const),
        ],
        out_specs=pl.BlockSpec((tm, D_MODEL), lambda i: (i, 0)),
        out_shape=jax.ShapeDtypeStruct((seq, D_MODEL), F32),
        scratch_shapes=[pltpu.VMEM((tm, CONV_WIDTH), BF16), pltpu.VMEM((tm, D_MODEL), F32)],
        compiler_params=pltpu.CompilerParams(
            dimension_semantics=("arbitrary",),
            vmem_limit_bytes=56 * _MIB),
        name="out_proj",
    )(attn_n, pc, pc, pc, pc, pc, pc_meta, pc_meta, x, conv_w, g_conv_out, w_bf16)


def _ffn_kernel(h_ref, g_ref, wg_ref, wu_ref, wd_ref, o_ref, n_ref):
    f = pl.program_id(1)

    @pl.when(f == 0)
    def _():
        h = h_ref[...]
        n_ref[...] = _rms(h, g_ref[...]).astype(BF16)
        o_ref[...] = h

    n = n_ref[...]
    gate = jnp.dot(n, wg_ref[0], preferred_element_type=F32)
    up = jnp.dot(n, wu_ref[0], preferred_element_type=F32)
    act = (gate * (1.0 / (1.0 + jnp.exp(-gate))) * up).astype(BF16)
    o_ref[...] += jnp.dot(act, wd_ref[...], preferred_element_type=F32)


def _ffn(h1, g_ffn, wg, wu, wd, *, tm):
    seq = h1.shape[0]
    n_f, _, tf = wg.shape
    assert wd.shape[0] == n_f * tf
    return pl.pallas_call(
        _ffn_kernel,
        grid=(seq // tm, n_f),
        in_specs=[
            pl.BlockSpec((tm, D_MODEL), lambda i, f: (i, 0)),
            pl.BlockSpec((1, D_MODEL), lambda i, f: (0, 0)),
            pl.BlockSpec((1, D_MODEL, tf), lambda i, f: (f, 0, 0)),
            pl.BlockSpec((1, D_MODEL, tf), lambda i, f: (f, 0, 0)),
            pl.BlockSpec((tf, D_MODEL), lambda i, f: (f, 0)),
        ],
        out_specs=pl.BlockSpec((tm, D_MODEL), lambda i, f: (i, 0)),
        out_shape=jax.ShapeDtypeStruct((seq, D_MODEL), F32),
        scratch_shapes=[pltpu.VMEM((tm, D_MODEL), BF16)],
        compiler_params=pltpu.CompilerParams(
            dimension_semantics=("arbitrary", "arbitrary"),
            vmem_limit_bytes=60 * _MIB),
        name="ffn",
    )(h1, g_ffn, wg, wu, wd)


def kernel(x, meta_tokens, g_mix, w_in, g_q, g_k, conv_w, g_attn_out, g_conv_out,
           w_out, g_ffn, w_gate, w_up, w_down):
    batch, seq, _ = x.shape
    depth = w_in.shape[0]
    assert batch == 1 and depth == 1
    xs = x[0]
    tq = 256
    tf = 512

    qkv_meta, pc_meta, w_in_b = _in_proj(meta_tokens, g_mix, w_in[0], g_q, g_k, tm=N_META)
    qkv, pc, w_out_b = _in_proj(xs, g_mix, w_in_b, g_q, g_k, side=((w_out[0], None),), tm=1024)
    qkv_meta_pad = jnp.pad(qkv_meta, ((0, 0), (0, tq - N_META), (0, 0)))

    attn_n, w_gate_b, w_up_b, w_down_b = _attention(
        qkv, qkv_meta_pad, g_attn_out,
        side=((w_gate[0], tf), (w_up[0], tf), (w_down[0], None)), tq=tq, hpg=N_HEADS)
    h1 = _out_proj(attn_n, pc, pc_meta, xs, conv_w[0], g_conv_out, w_out_b, tm=512)
    out = _ffn(h1, g_ffn, w_gate_b, w_up_b, w_down_b, tm=1024)
    return out[None]
```

```python
import functools
import math

import jax
import jax.numpy as jnp
from jax import lax
from jax.experimental import pallas as pl
from jax.experimental.pallas import tpu as pltpu

D_MODEL = 2048
N_META = 16
ATTN_WIDTH = 1024
CONV_WIDTH = 1024
HEAD_DIM = 128
N_HEADS = ATTN_WIDTH // HEAD_DIM
CONV_K = 3
EPS = 1e-6

F32 = jnp.float32
BF16 = jnp.bfloat16

_N_SEG = 6
_N_ATTN_SEG = 3
_SEG_Q, _SEG_K, _SEG_V = 0, 1, 2
_CSEG_GB, _CSEG_GC, _CSEG_U = 0, 1, 2

_MIB = 1024 * 1024

_SKIP = 110.0


def _rms(x, g):
    return x * lax.rsqrt(jnp.mean(x * x, axis=-1, keepdims=True) + EPS) * g


_N_CHUNK = 4
_CHUNK = ATTN_WIDTH // _N_CHUNK
_HEADS_PER_CHUNK = _CHUNK // HEAD_DIM
_CAST_STEPS = 32
_SPLIT_ALIGN = 256


def _cast_specs(a, chunk, step_of):
    rows = a.shape[0] // _CAST_STEPS
    assert rows * _CAST_STEPS == a.shape[0] and rows % 16 == 0

    def step(*idx):
        return jnp.minimum(step_of(*idx), _CAST_STEPS - 1)
    in_spec = pl.BlockSpec((rows, a.shape[1]), lambda *idx: (step(*idx), 0))
    if chunk is None:
        return in_spec, in_spec, jax.ShapeDtypeStruct(a.shape, BF16)
    n = a.shape[1] // chunk
    assert n * chunk == a.shape[1]
    out_spec = pl.BlockSpec((n, rows, chunk), lambda *idx: (0, step(*idx), 0))
    return in_spec, out_spec, jax.ShapeDtypeStruct((n, a.shape[0], chunk), BF16)


def _cast_side(side_in, side_out):
    for src, dst in zip(side_in, side_out):
        if len(dst.shape) == 2:
            dst[...] = src[...].astype(BF16)
        else:
            n, _, chunk = dst.shape
            for f in range(n):
                dst[f] = src[:, f * chunk:(f + 1) * chunk].astype(BF16)


def _in_proj_kernel(*refs, cast_w, n_side):
    x_ref, g_ref, w_ref, gq_ref, gk_ref = refs[:5]
    side_in = refs[5:5 + n_side]
    outs = refs[5 + n_side:]
    qkv_ref, pc_ref = outs[:2]
    outs = outs[2:]
    if cast_w:
        wb_ref, outs = outs[0], outs[1:]
    side_out, xn_ref = outs[:n_side], outs[n_side]
    j = pl.program_id(1)
    tm = x_ref.shape[0]
    all_rows = slice(0, tm)

    def chunks(epilogue, rows=all_rows, xn=None, first=True):
        if first:
            _cast_side(side_in, side_out)
        if xn is None:
            xn = xn_ref[rows]
        for c in range(_N_CHUNK):
            cols = slice(c * _CHUNK, (c + 1) * _CHUNK)
            if cast_w:
                if first:
                    wb_ref[0, :, cols] = w_ref[:, cols].astype(BF16)
                w = wb_ref[0, :, cols]
            else:
                w = w_ref[0, :, cols]
            epilogue(c, rows, jnp.dot(xn, w, preferred_element_type=F32))

    def heads_out(fn):
        def epilogue(c, rows, acc):
            for h in range(_HEADS_PER_CHUNK):
                hd = c * _HEADS_PER_CHUNK + h
                a = acc[:, h * HEAD_DIM:(h + 1) * HEAD_DIM]
                if qkv_ref.shape[1] != tm:
                    qkv_ref[hd] = jnp.zeros(qkv_ref.shape[1:], BF16)
                qkv_ref[hd, rows] = fn(a).astype(BF16)
        return epilogue

    @pl.when(j == _SEG_Q)
    def _():
        gain = gq_ref[...] * (1.0 / math.sqrt(HEAD_DIM))
        n_split = 2 if tm % (2 * _SPLIT_ALIGN) == 0 else 1
        for r in range(n_split):
            rows = slice(r * tm // n_split, (r + 1) * tm // n_split)
            xn = _rms(x_ref[rows], g_ref[...]).astype(BF16)
            xn_ref[rows] = xn
            chunks(heads_out(lambda a: _rms(a, gain)), rows, xn, first=(r == 0))

    @pl.when(j == _SEG_K)
    def _():
        gain = gk_ref[...]
        chunks(heads_out(lambda a: _rms(a, gain)))

    @pl.when(j == _SEG_V)
    def _():
        chunks(heads_out(lambda a: a))

    @pl.when(j >= _N_ATTN_SEG)
    def _():
        def epilogue(c, rows, acc):
            pc_ref[rows, c * _CHUNK:(c + 1) * _CHUNK] = acc.astype(BF16)
        chunks(epilogue)


def _in_proj(x, g_mix, w, g_q, g_k, side=(), *, tm, qkv_rows=None):
    m = x.shape[0]
    tn = ATTN_WIDTH
    n_i = m // tm
    cast_w = w.dtype != BF16
    if qkv_rows is None:
        qkv_rows = tm
    assert qkv_rows == tm or n_i == 1

    assert not side or n_i * _N_SEG >= _CAST_STEPS
    casts = [_cast_specs(a, chunk, lambda i, j: i * _N_SEG + j) for a, chunk in side]
    w_seg_spec = pl.BlockSpec((1, D_MODEL, tn), lambda i, j: (j, 0, 0))
    out_specs = [
        pl.BlockSpec((N_HEADS, qkv_rows, HEAD_DIM),
                     lambda i, j: (jnp.minimum(j, _N_ATTN_SEG - 1), i, 0)),
        pl.BlockSpec((tm, CONV_WIDTH),
                     lambda i, j: (i, jnp.maximum(j - _N_ATTN_SEG, 0))),
    ]
    out_shape = [
        jax.ShapeDtypeStruct((_N_ATTN_SEG * N_HEADS, n_i * qkv_rows, HEAD_DIM), BF16),
        jax.ShapeDtypeStruct((m, (_N_SEG - _N_ATTN_SEG) * CONV_WIDTH), BF16),
    ]
    if cast_w:
        assert n_i == 1
        out_specs.append(w_seg_spec)
        out_shape.append(jax.ShapeDtypeStruct((_N_SEG, D_MODEL, tn), BF16))
        w_spec = pl.BlockSpec((D_MODEL, tn), lambda i, j: (0, j))
    else:
        w_spec = w_seg_spec
    out_specs += [c[1] for c in casts]
    out_shape += [c[2] for c in casts]
    return pl.pallas_call(
        functools.partial(_in_proj_kernel, cast_w=cast_w, n_side=len(side)),
        grid=(n_i, _N_SEG),
        in_specs=[
            pl.BlockSpec((tm, D_MODEL), lambda i, j: (i, 0)),
            pl.BlockSpec((1, D_MODEL), lambda i, j: (0, 0)),
            w_spec,
            pl.BlockSpec((1, HEAD_DIM), lambda i, j: (0, 0)),
            pl.BlockSpec((1, HEAD_DIM), lambda i, j: (0, 0)),
        ] + [c[0] for c in casts],
        out_specs=out_specs,
        out_shape=out_shape,
        scratch_shapes=[pltpu.VMEM((tm, D_MODEL), BF16)],
        compiler_params=pltpu.CompilerParams(
            dimension_semantics=("arbitrary", "arbitrary"),
            vmem_limit_bytes=56 * _MIB),
        name="in_proj",
    )(x, g_mix, w, g_q, g_k, *[a for a, _ in side])


_MASKED = -1e30


def _attn_blocks(qs, kbs, vbs, tri, masks, accs, carries):
    n, nb = len(qs), len(masks)
    ss = [[lax.dot_general(qs[c], kbs[c][b], (((1,), (1,)), ((), ())),
                           preferred_element_type=F32) for b in range(nb)] for c in range(n)]
    parts = []
    for c in range(n):
        row = []
        for b in range(nb):
            s = ss[c][b] if masks[b] is None else jnp.where(masks[b], ss[c][b], _MASKED)
            sp = jnp.maximum(s, 0.0) + jnp.log(1.0 + jnp.exp(-jnp.abs(s)))
            lb = s - sp
            hi = sp.astype(BF16)
            lo = (sp - hi.astype(F32)).astype(BF16)
            cum = (jnp.dot(hi, tri, preferred_element_type=F32)
                   + jnp.dot(lo, tri, preferred_element_type=F32))
            row.append((sp, lb, cum))
        parts.append(row)
    out = []
    for c in range(n):
        acc, carry = accs[c], carries[c]
        for b, (sp, lb, cum) in enumerate(parts[c]):
            w = jnp.exp(lb - cum - carry)
            acc = acc + jnp.dot(w.astype(BF16), vbs[c][b], preferred_element_type=F32)
            carry = carry + jnp.sum(sp, axis=1, keepdims=True)
        out.append((acc, carry))
    return out


def _attn_kernel(*refs, tq, n_side):
    q_ref, k_ref, v_ref, km_ref, vm_ref, g_ref = refs[:6]
    side_in = refs[6:6 + n_side]
    o_ref = refs[6 + n_side]
    side_out = refs[7 + n_side:7 + 2 * n_side]
    acc_ref, car_ref = refs[7 + 2 * n_side:]
    heads = range(N_HEADS)
    _cast_side(side_in, side_out)

    i = pl.program_id(0)
    row = lax.broadcasted_iota(jnp.int32, (tq, tq), 0)
    col = lax.broadcasted_iota(jnp.int32, (tq, tq), 1)
    tri = (row > col).astype(BF16)
    diag_mask = col < row
    meta_mask = col < N_META
    base = pl.multiple_of(i * tq, tq)

    def sweep(blocks, first):
        if first:
            init = [(jnp.zeros((tq, HEAD_DIM), F32), jnp.zeros((tq, 1), F32))] * N_HEADS
        else:
            init = [(acc_ref[h], car_ref[h]) for h in heads]

        def block(ref, meta_ref, h, off):
            return meta_ref[h] if off is None else ref[h, pl.ds(off, tq), :]
        kbs = [[block(k_ref, km_ref, h, off) for off, _ in blocks] for h in heads]
        vbs = [[block(v_ref, vm_ref, h, off) for off, _ in blocks] for h in heads]
        res = _attn_blocks([q_ref[h] for h in heads], kbs, vbs, tri, [m for _, m in blocks],
                           [a for a, _ in init], [c for _, c in init])
        cmin = None
        for h, (acc, car) in enumerate(res):
            acc_ref[h] = acc
            car_ref[h] = car
            m = jnp.min(car)
            cmin = m if cmin is None else jnp.minimum(cmin, m)
        return cmin

    cmin = sweep([(base, diag_mask)], True)

    def cond(st):
        return jnp.logical_and(st[0] < i, st[1] < _SKIP)

    def body(st):
        off = pl.multiple_of((i - 1 - st[0]) * tq, tq)
        return st[0] + 1, sweep([(off, None)], False)

    _, cmin = lax.while_loop(cond, body, (jnp.int32(0), cmin))

    @pl.when(cmin < _SKIP)
    def _():
        sweep([(None, meta_mask)], False)

    ssq = jnp.zeros((tq, HEAD_DIM), F32)
    for h in heads:
        o = acc_ref[h]
        ssq = ssq + o * o
    inv = lax.rsqrt(jnp.sum(ssq, axis=1, keepdims=True) * (1.0 / ATTN_WIDTH) + EPS)
    for h in heads:
        sl = slice(h * HEAD_DIM, (h + 1) * HEAD_DIM)
        o_ref[:, sl] = (acc_ref[h] * inv * g_ref[:, sl]).astype(BF16)


def _attention(qkv, qkv_meta_pad, g_attn_out, side=(), *, tq):
    seq = qkv.shape[1]
    mpad = qkv_meta_pad.shape[1]
    assert mpad == tq
    assert not side or seq // tq >= _CAST_STEPS
    resident = dict(pipeline_mode=pl.Buffered(1))
    casts = [_cast_specs(a, chunk, lambda i: i) for a, chunk in side]
    return pl.pallas_call(
        functools.partial(_attn_kernel, tq=tq, n_side=len(side)),
        grid=(seq // tq,),
        in_specs=[
            pl.BlockSpec((N_HEADS, tq, HEAD_DIM), lambda i: (_SEG_Q, i, 0)),
            pl.BlockSpec((N_HEADS, seq, HEAD_DIM), lambda i: (_SEG_K, 0, 0), **resident),
            pl.BlockSpec((N_HEADS, seq, HEAD_DIM), lambda i: (_SEG_V, 0, 0), **resident),
            pl.BlockSpec((N_HEADS, mpad, HEAD_DIM), lambda i: (_SEG_K, 0, 0), **resident),
            pl.BlockSpec((N_HEADS, mpad, HEAD_DIM), lambda i: (_SEG_V, 0, 0), **resident),
            pl.BlockSpec((1, ATTN_WIDTH), lambda i: (0, 0)),
        ] + [c[0] for c in casts],
        out_specs=[pl.BlockSpec((tq, ATTN_WIDTH), lambda i: (i, 0))] + [c[1] for c in casts],
        out_shape=[jax.ShapeDtypeStruct((seq, ATTN_WIDTH), BF16)] + [c[2] for c in casts],
        scratch_shapes=[
            pltpu.VMEM((N_HEADS, tq, HEAD_DIM), F32),
            pltpu.VMEM((N_HEADS, tq, 1), F32),
        ],
        compiler_params=pltpu.CompilerParams(
            dimension_semantics=("arbitrary",),
            vmem_limit_bytes=60 * _MIB),
        name="attn",
    )(qkv, qkv, qkv, qkv_meta_pad, qkv_meta_pad, g_attn_out, *[a for a, _ in side])


_HALO = 16
_CONV_ROWS = 64
_CONV_COLS = 256


def _out_proj_kernel(an_ref, gb_ref, gc_ref, u_ref, gch_ref, uh_ref, gcm_ref, um_ref,
                     x_ref, cw_ref, gcv_ref, w_ref, h_ref, ocg_ref, hc_ref):
    i = pl.program_id(0)
    tm = an_ref.shape[0]
    n_rows = tm // _CONV_ROWS
    n_cols = CONV_WIDTH // _CONV_COLS
    nw = D_MODEL // n_cols
    an = an_ref[...]
    ssq = [jnp.zeros((_CONV_ROWS, 128), F32)] * n_rows
    for c in range(n_cols):
        c0 = c * _CONV_COLS
        cols = slice(c0, c0 + _CONV_COLS)
        ocols = slice(c * nw, (c + 1) * nw)
        h_ref[:, ocols] = x_ref[:, ocols] + jnp.dot(an, w_ref[:ATTN_WIDTH, ocols],
                                                    preferred_element_type=F32)
        for r in range(n_rows):
            r0 = r * _CONV_ROWS
            if r0 == 0:
                h_seq = gch_ref[:, cols].astype(F32) * uh_ref[:, cols].astype(F32)
                h_meta = gcm_ref[:, cols].astype(F32) * um_ref[:, cols].astype(F32)
                halo = jnp.where(i == 0, h_meta, h_seq)
                cu = gc_ref[:_CONV_ROWS, cols].astype(F32) * u_ref[:_CONV_ROWS, cols].astype(F32)
                ext = jnp.concatenate([halo, cu], axis=0)
            else:
                rows = slice(r0 - _HALO, r0 + _CONV_ROWS)
                ext = gc_ref[rows, cols].astype(F32) * u_ref[rows, cols].astype(F32)
            cw = cw_ref[:, cols]
            y = (cw[0:1] * pltpu.roll(ext, 2, axis=0)[_HALO:]
                 + cw[1:2] * pltpu.roll(ext, 1, axis=0)[_HALO:]
                 + cw[2:3] * ext[_HALO:])
            oc = gb_ref[r0:r0 + _CONV_ROWS, cols].astype(F32) * y
            sq = oc * oc
            for l0 in range(0, _CONV_COLS, 128):
                ssq[r] = ssq[r] + sq[:, l0:l0 + 128]
            ocg_ref[r0:r0 + _CONV_ROWS, cols] = (oc * gcv_ref[:, cols]).astype(BF16)
        part = jnp.dot(ocg_ref[:, cols], w_ref[ATTN_WIDTH + c0:ATTN_WIDTH + c0 + _CONV_COLS, :],
                       preferred_element_type=F32)
        if c == 0:
            hc_ref[...] = part
        else:
            hc_ref[...] += part
    inv = jnp.concatenate(
        [lax.rsqrt(jnp.sum(s, axis=1, keepdims=True) * (1.0 / CONV_WIDTH) + EPS) for s in ssq],
        axis=0)
    h_ref[...] += inv * hc_ref[...]


def _out_proj(attn_n, pc, pc_meta, x, conv_w, g_conv_out, w_bf16, *, tm):
    seq = x.shape[0]
    hb = tm // _HALO
    seg = lambda s: (lambda i: (i, s))
    halo = lambda s: (lambda i: (jnp.maximum(i * hb - 1, 0), s))
    meta = lambda s: (lambda i: (0, s))
    const = lambda i: (0, 0)
    return pl.pallas_call(
        _out_proj_kernel,
        grid=(seq // tm,),
        in_specs=[
            pl.BlockSpec((tm, ATTN_WIDTH), lambda i: (i, 0)),
            pl.BlockSpec((tm, CONV_WIDTH), seg(_CSEG_GB)),
            pl.BlockSpec((tm, CONV_WIDTH), seg(_CSEG_GC)),
            pl.BlockSpec((tm, CONV_WIDTH), seg(_CSEG_U)),
            pl.BlockSpec((_HALO, CONV_WIDTH), halo(_CSEG_GC)),
            pl.BlockSpec((_HALO, CONV_WIDTH), halo(_CSEG_U)),
            pl.BlockSpec((N_META, CONV_WIDTH), meta(_CSEG_GC)),
            pl.BlockSpec((N_META, CONV_WIDTH), meta(_CSEG_U)),
            pl.BlockSpec((tm, D_MODEL), lambda i: (i, 0)),
            pl.BlockSpec((CONV_K, CONV_WIDTH), const),
            pl.BlockSpec((1, CONV_WIDTH), const),
            pl.BlockSpec((D_MODEL, D_MODEL), const),
        ],
        out_specs=pl.BlockSpec((tm, D_MODEL), lambda i: (i, 0)),
        out_shape=jax.ShapeDtypeStruct((seq, D_MODEL), F32),
        scratch_shapes=[pltpu.VMEM((tm, CONV_WIDTH), BF16), pltpu.VMEM((tm, D_MODEL), F32)],
        compiler_params=pltpu.CompilerParams(
            dimension_semantics=("arbitrary",),
            vmem_limit_bytes=56 * _MIB),
        name="out_proj",
    )(attn_n, pc, pc, pc, pc, pc, pc_meta, pc_meta, x, conv_w, g_conv_out, w_bf16)


def _ffn_kernel(h_ref, g_ref, wg_ref, wu_ref, wd_ref, o_ref, n_ref):
    f = pl.program_id(1)
    tm = h_ref.shape[0]

    def mlp(n):
        gate = jnp.dot(n, wg_ref[0], preferred_element_type=F32)
        up = jnp.dot(n, wu_ref[0], preferred_element_type=F32)
        act = (gate * (1.0 / (1.0 + jnp.exp(-gate))) * up).astype(BF16)
        return jnp.dot(act, wd_ref[...], preferred_element_type=F32)

    @pl.when(f == 0)
    def _():
        n_split = 2 if tm % (2 * _SPLIT_ALIGN) == 0 else 1
        for r in range(n_split):
            rows = slice(r * tm // n_split, (r + 1) * tm // n_split)
            h = h_ref[rows]
            n = _rms(h, g_ref[...]).astype(BF16)
            n_ref[rows] = n
            o_ref[rows] = h + mlp(n)

    @pl.when(f > 0)
    def _():
        o_ref[...] += mlp(n_ref[...])


def _ffn(h1, g_ffn, wg, wu, wd, *, tm):
    seq = h1.shape[0]
    n_f, _, tf = wg.shape
    assert wd.shape[0] == n_f * tf
    return pl.pallas_call(
        _ffn_kernel,
        grid=(seq // tm, n_f),
        in_specs=[
            pl.BlockSpec((tm, D_MODEL), lambda i, f: (i, 0)),
            pl.BlockSpec((1, D_MODEL), lambda i, f: (0, 0)),
            pl.BlockSpec((1, D_MODEL, tf), lambda i, f: (f, 0, 0)),
            pl.BlockSpec((1, D_MODEL, tf), lambda i, f: (f, 0, 0)),
            pl.BlockSpec((tf, D_MODEL), lambda i, f: (f, 0)),
        ],
        out_specs=pl.BlockSpec((tm, D_MODEL), lambda i, f: (i, 0)),
        out_shape=jax.ShapeDtypeStruct((seq, D_MODEL), F32),
        scratch_shapes=[pltpu.VMEM((tm, D_MODEL), BF16)],
        compiler_params=pltpu.CompilerParams(
            dimension_semantics=("arbitrary", "arbitrary"),
            vmem_limit_bytes=60 * _MIB),
        name="ffn",
    )(h1, g_ffn, wg, wu, wd)


def kernel(x, meta_tokens, g_mix, w_in, g_q, g_k, conv_w, g_attn_out, g_conv_out,
           w_out, g_ffn, w_gate, w_up, w_down):
    batch, seq, _ = x.shape
    depth = w_in.shape[0]
    assert batch == 1 and depth == 1
    xs = x[0]
    tq = 256
    tf = 512

    qkv_meta_pad, pc_meta, w_in_b = _in_proj(meta_tokens, g_mix, w_in[0], g_q, g_k,
                                             tm=N_META, qkv_rows=tq)
    qkv, pc, w_out_b = _in_proj(xs, g_mix, w_in_b, g_q, g_k, side=((w_out[0], None),), tm=1024)

    attn_n, w_gate_b, w_up_b, w_down_b = _attention(
        qkv, qkv_meta_pad, g_attn_out,
        side=((w_gate[0], tf), (w_up[0], tf), (w_down[0], None)), tq=tq)
    h1 = _out_proj(attn_n, pc, pc_meta, xs, conv_w[0], g_conv_out, w_out_b, tm=512)
    out = _ffn(h1, g_ffn, w_gate_b, w_up_b, w_down_b, tm=1024)
    return out[None]
```

```python
import functools
import math

import jax
import jax.numpy as jnp
from jax import lax
from jax.experimental import pallas as pl
from jax.experimental.pallas import tpu as pltpu

D_MODEL = 2048
N_META = 16
ATTN_WIDTH = 1024
CONV_WIDTH = 1024
HEAD_DIM = 128
N_HEADS = ATTN_WIDTH // HEAD_DIM
CONV_K = 3
EPS = 1e-6

F32 = jnp.float32
BF16 = jnp.bfloat16

_N_SEG = 6
_N_ATTN_SEG = 3
_SEG_Q, _SEG_K, _SEG_V = 0, 1, 2
_CSEG_GB, _CSEG_GC, _CSEG_U = 0, 1, 2

_MIB = 1024 * 1024

_SKIP = 110.0


def _rms(x, g):
    return x * lax.rsqrt(jnp.mean(x * x, axis=-1, keepdims=True) + EPS) * g


_N_CHUNK = 4
_CHUNK = ATTN_WIDTH // _N_CHUNK
_HEADS_PER_CHUNK = _CHUNK // HEAD_DIM
_CAST_STEPS = 32
_SPLIT_ALIGN = 256


def _cast_specs(a, chunk, step_of):
    rows = a.shape[0] // _CAST_STEPS
    assert rows * _CAST_STEPS == a.shape[0] and rows % 16 == 0

    def step(*idx):
        return jnp.minimum(step_of(*idx), _CAST_STEPS - 1)
    in_spec = pl.BlockSpec((rows, a.shape[1]), lambda *idx: (step(*idx), 0))
    if chunk is None:
        return in_spec, in_spec, jax.ShapeDtypeStruct(a.shape, BF16)
    n = a.shape[1] // chunk
    assert n * chunk == a.shape[1]
    out_spec = pl.BlockSpec((n, rows, chunk), lambda *idx: (0, step(*idx), 0))
    return in_spec, out_spec, jax.ShapeDtypeStruct((n, a.shape[0], chunk), BF16)


def _cast_side(side_in, side_out):
    for src, dst in zip(side_in, side_out):
        if len(dst.shape) == 2:
            dst[...] = src[...].astype(BF16)
        else:
            n, _, chunk = dst.shape
            for f in range(n):
                dst[f] = src[:, f * chunk:(f + 1) * chunk].astype(BF16)


def _in_proj_kernel(*refs, cast_w, n_side):
    x_ref, g_ref, w_ref, gq_ref, gk_ref = refs[:5]
    side_in = refs[5:5 + n_side]
    outs = refs[5 + n_side:]
    qkv_ref, pc_ref = outs[:2]
    outs = outs[2:]
    if cast_w:
        wb_ref, outs = outs[0], outs[1:]
    side_out, xn_ref = outs[:n_side], outs[n_side]
    j = pl.program_id(1)
    tm = x_ref.shape[0]
    all_rows = slice(0, tm)

    def chunks(epilogue, rows=all_rows, xn=None, first=True):
        if first:
            _cast_side(side_in, side_out)
        if xn is None:
            xn = xn_ref[rows]
        for c in range(_N_CHUNK):
            cols = slice(c * _CHUNK, (c + 1) * _CHUNK)
            if cast_w:
                if first:
                    wb_ref[0, :, cols] = w_ref[:, cols].astype(BF16)
                w = wb_ref[0, :, cols]
            else:
                w = w_ref[0, :, cols]
            epilogue(c, rows, jnp.dot(xn, w, preferred_element_type=F32))

    def heads_out(fn):
        def epilogue(c, rows, acc):
            for h in range(_HEADS_PER_CHUNK):
                hd = c * _HEADS_PER_CHUNK + h
                a = acc[:, h * HEAD_DIM:(h + 1) * HEAD_DIM]
                if qkv_ref.shape[1] != tm:
                    qkv_ref[hd] = jnp.zeros(qkv_ref.shape[1:], BF16)
                qkv_ref[hd, rows] = fn(a).astype(BF16)
        return epilogue

    @pl.when(j == _SEG_Q)
    def _():
        gain = gq_ref[...] * (1.0 / math.sqrt(HEAD_DIM))
        n_split = 2 if tm % (2 * _SPLIT_ALIGN) == 0 else 1
        for r in range(n_split):
            rows = slice(r * tm // n_split, (r + 1) * tm // n_split)
            xn = _rms(x_ref[rows], g_ref[...]).astype(BF16)
            xn_ref[rows] = xn
            chunks(heads_out(lambda a: _rms(a, gain)), rows, xn, first=(r == 0))

    @pl.when(j == _SEG_K)
    def _():
        gain = gk_ref[...]
        chunks(heads_out(lambda a: _rms(a, gain)))

    @pl.when(j == _SEG_V)
    def _():
        chunks(heads_out(lambda a: a))

    @pl.when(j >= _N_ATTN_SEG)
    def _():
        def epilogue(c, rows, acc):
            pc_ref[rows, c * _CHUNK:(c + 1) * _CHUNK] = acc.astype(BF16)
        chunks(epilogue)


def _in_proj(x, g_mix, w, g_q, g_k, side=(), *, tm, qkv_rows=None):
    m = x.shape[0]
    tn = ATTN_WIDTH
    n_i = m // tm
    cast_w = w.dtype != BF16
    if qkv_rows is None:
        qkv_rows = tm
    assert qkv_rows == tm or n_i == 1

    assert not side or n_i * _N_SEG >= _CAST_STEPS
    casts = [_cast_specs(a, chunk, lambda i, j: i * _N_SEG + j) for a, chunk in side]
    w_seg_spec = pl.BlockSpec((1, D_MODEL, tn), lambda i, j: (j, 0, 0))
    out_specs = [
        pl.BlockSpec((N_HEADS, qkv_rows, HEAD_DIM),
                     lambda i, j: (jnp.minimum(j, _N_ATTN_SEG - 1), i, 0)),
        pl.BlockSpec((tm, CONV_WIDTH),
                     lambda i, j: (i, jnp.maximum(j - _N_ATTN_SEG, 0))),
    ]
    out_shape = [
        jax.ShapeDtypeStruct((_N_ATTN_SEG * N_HEADS, n_i * qkv_rows, HEAD_DIM), BF16),
        jax.ShapeDtypeStruct((m, (_N_SEG - _N_ATTN_SEG) * CONV_WIDTH), BF16),
    ]
    if cast_w:
        assert n_i == 1
        out_specs.append(w_seg_spec)
        out_shape.append(jax.ShapeDtypeStruct((_N_SEG, D_MODEL, tn), BF16))
        w_spec = pl.BlockSpec((D_MODEL, tn), lambda i, j: (0, j))
    else:
        w_spec = w_seg_spec
    out_specs += [c[1] for c in casts]
    out_shape += [c[2] for c in casts]
    return pl.pallas_call(
        functools.partial(_in_proj_kernel, cast_w=cast_w, n_side=len(side)),
        grid=(n_i, _N_SEG),
        in_specs=[
            pl.BlockSpec((tm, D_MODEL),
                         lambda i, j: (jnp.minimum(i + jnp.minimum(j, 1), n_i - 1), 0)),
            pl.BlockSpec((1, D_MODEL), lambda i, j: (0, 0)),
            w_spec,
            pl.BlockSpec((1, HEAD_DIM), lambda i, j: (0, 0)),
            pl.BlockSpec((1, HEAD_DIM), lambda i, j: (0, 0)),
        ] + [c[0] for c in casts],
        out_specs=out_specs,
        out_shape=out_shape,
        scratch_shapes=[pltpu.VMEM((tm, D_MODEL), BF16)],
        compiler_params=pltpu.CompilerParams(
            dimension_semantics=("arbitrary", "arbitrary"),
            vmem_limit_bytes=56 * _MIB),
        name="in_proj",
    )(x, g_mix, w, g_q, g_k, *[a for a, _ in side])


_MASKED = -1e30


def _attn_blocks(qs, kbs, vbs, tri, masks, accs, carries):
    n, nb = len(qs), len(masks)
    ss = [[lax.dot_general(qs[c], kbs[c][b], (((1,), (1,)), ((), ())),
                           preferred_element_type=F32) for b in range(nb)] for c in range(n)]
    parts = []
    for c in range(n):
        row = []
        for b in range(nb):
            s = ss[c][b] if masks[b] is None else jnp.where(masks[b], ss[c][b], _MASKED)
            sp = jnp.maximum(s, 0.0) + jnp.log(1.0 + jnp.exp(-jnp.abs(s)))
            lb = s - sp
            hi = sp.astype(BF16)
            lo = (sp - hi.astype(F32)).astype(BF16)
            cum = (jnp.dot(hi, tri, preferred_element_type=F32)
                   + jnp.dot(lo, tri, preferred_element_type=F32))
            row.append((sp, lb, cum))
        parts.append(row)
    out = []
    for c in range(n):
        acc, carry = accs[c], carries[c]
        for b, (sp, lb, cum) in enumerate(parts[c]):
            w = jnp.exp(lb - cum - carry)
            acc = acc + jnp.dot(w.astype(BF16), vbs[c][b], preferred_element_type=F32)
            carry = carry + jnp.sum(sp, axis=1, keepdims=True)
        out.append((acc, carry))
    return out


def _attn_kernel(*refs, tq, n_side):
    q_ref, k_ref, v_ref, km_ref, vm_ref, g_ref = refs[:6]
    side_in = refs[6:6 + n_side]
    o_ref = refs[6 + n_side]
    side_out = refs[7 + n_side:7 + 2 * n_side]
    acc_ref, car_ref = refs[7 + 2 * n_side:]
    heads = range(N_HEADS)
    _cast_side(side_in, side_out)

    i = pl.program_id(0)
    row = lax.broadcasted_iota(jnp.int32, (tq, tq), 0)
    col = lax.broadcasted_iota(jnp.int32, (tq, tq), 1)
    tri = (row > col).astype(BF16)
    diag_mask = col < row
    meta_mask = col < N_META
    base = pl.multiple_of(i * tq, tq)

    def sweep(blocks, first):
        if first:
            init = [(jnp.zeros((tq, HEAD_DIM), F32), jnp.zeros((tq, 1), F32))] * N_HEADS
        else:
            init = [(acc_ref[h], car_ref[h]) for h in heads]

        def block(ref, meta_ref, h, off):
            return meta_ref[h] if off is None else ref[h, pl.ds(off, tq), :]
        kbs = [[block(k_ref, km_ref, h, off) for off, _ in blocks] for h in heads]
        vbs = [[block(v_ref, vm_ref, h, off) for off, _ in blocks] for h in heads]
        res = _attn_blocks([q_ref[h] for h in heads], kbs, vbs, tri, [m for _, m in blocks],
                           [a for a, _ in init], [c for _, c in init])
        cmin = None
        for h, (acc, car) in enumerate(res):
            acc_ref[h] = acc
            car_ref[h] = car
            m = jnp.min(car)
            cmin = m if cmin is None else jnp.minimum(cmin, m)
        return cmin

    cmin = sweep([(base, diag_mask)], True)

    def cond(st):
        return jnp.logical_and(st[0] < i, st[1] < _SKIP)

    def body(st):
        off = pl.multiple_of((i - 1 - st[0]) * tq, tq)
        return st[0] + 1, sweep([(off, None)], False)

    _, cmin = lax.while_loop(cond, body, (jnp.int32(0), cmin))

    @pl.when(cmin < _SKIP)
    def _():
        sweep([(None, meta_mask)], False)

    ssq = jnp.zeros((tq, HEAD_DIM), F32)
    for h in heads:
        o = acc_ref[h]
        ssq = ssq + o * o
    inv = lax.rsqrt(jnp.sum(ssq, axis=1, keepdims=True) * (1.0 / ATTN_WIDTH) + EPS)
    for h in heads:
        sl = slice(h * HEAD_DIM, (h + 1) * HEAD_DIM)
        o_ref[:, sl] = (acc_ref[h] * inv * g_ref[:, sl]).astype(BF16)


def _attention(qkv, qkv_meta_pad, g_attn_out, side=(), *, tq):
    seq = qkv.shape[1]
    mpad = qkv_meta_pad.shape[1]
    assert mpad == tq
    assert not side or seq // tq >= _CAST_STEPS
    resident = dict(pipeline_mode=pl.Buffered(1))
    casts = [_cast_specs(a, chunk, lambda i: i) for a, chunk in side]
    return pl.pallas_call(
        functools.partial(_attn_kernel, tq=tq, n_side=len(side)),
        grid=(seq // tq,),
        in_specs=[
            pl.BlockSpec((N_HEADS, tq, HEAD_DIM), lambda i: (_SEG_Q, i, 0)),
            pl.BlockSpec((N_HEADS, seq, HEAD_DIM), lambda i: (_SEG_K, 0, 0), **resident),
            pl.BlockSpec((N_HEADS, seq, HEAD_DIM), lambda i: (_SEG_V, 0, 0), **resident),
            pl.BlockSpec((N_HEADS, mpad, HEAD_DIM), lambda i: (_SEG_K, 0, 0), **resident),
            pl.BlockSpec((N_HEADS, mpad, HEAD_DIM), lambda i: (_SEG_V, 0, 0), **resident),
            pl.BlockSpec((1, ATTN_WIDTH), lambda i: (0, 0)),
        ] + [c[0] for c in casts],
        out_specs=[pl.BlockSpec((tq, ATTN_WIDTH), lambda i: (i, 0))] + [c[1] for c in casts],
        out_shape=[jax.ShapeDtypeStruct((seq, ATTN_WIDTH), BF16)] + [c[2] for c in casts],
        scratch_shapes=[
            pltpu.VMEM((N_HEADS, tq, HEAD_DIM), F32),
            pltpu.VMEM((N_HEADS, tq, 1), F32),
        ],
        compiler_params=pltpu.CompilerParams(
            dimension_semantics=("arbitrary",),
            vmem_limit_bytes=60 * _MIB),
        name="attn",
    )(qkv, qkv, qkv, qkv_meta_pad, qkv_meta_pad, g_attn_out, *[a for a, _ in side])


_HALO = 16
_CONV_ROWS = 64
_CONV_COLS = 256


def _out_proj_kernel(an_ref, gb_ref, gc_ref, u_ref, gch_ref, uh_ref, gcm_ref, um_ref,
                     x_ref, cw_ref, gcv_ref, w_ref, h_ref, ocg_ref, hc_ref):
    i = pl.program_id(0)
    tm = an_ref.shape[0]
    n_rows = tm // _CONV_ROWS
    n_cols = CONV_WIDTH // _CONV_COLS
    nw = D_MODEL // n_cols
    an = an_ref[...]
    ssq = [jnp.zeros((_CONV_ROWS, 128), F32)] * n_rows
    for c in range(n_cols):
        c0 = c * _CONV_COLS
        cols = slice(c0, c0 + _CONV_COLS)
        ocols = slice(c * nw, (c + 1) * nw)
        h_ref[:, ocols] = x_ref[:, ocols] + jnp.dot(an, w_ref[:ATTN_WIDTH, ocols],
                                                    preferred_element_type=F32)
        for r in range(n_rows):
            r0 = r * _CONV_ROWS
            if r0 == 0:
                h_seq = gch_ref[:, cols].astype(F32) * uh_ref[:, cols].astype(F32)
                h_meta = gcm_ref[:, cols].astype(F32) * um_ref[:, cols].astype(F32)
                halo = jnp.where(i == 0, h_meta, h_seq)
                cu = gc_ref[:_CONV_ROWS, cols].astype(F32) * u_ref[:_CONV_ROWS, cols].astype(F32)
                ext = jnp.concatenate([halo, cu], axis=0)
            else:
                rows = slice(r0 - _HALO, r0 + _CONV_ROWS)
                ext = gc_ref[rows, cols].astype(F32) * u_ref[rows, cols].astype(F32)
            cw = cw_ref[:, cols]
            y = (cw[0:1] * pltpu.roll(ext, 2, axis=0)[_HALO:]
                 + cw[1:2] * pltpu.roll(ext, 1, axis=0)[_HALO:]
                 + cw[2:3] * ext[_HALO:])
            oc = gb_ref[r0:r0 + _CONV_ROWS, cols].astype(F32) * y
            sq = oc * oc
            for l0 in range(0, _CONV_COLS, 128):
                ssq[r] = ssq[r] + sq[:, l0:l0 + 128]
            ocg_ref[r0:r0 + _CONV_ROWS, cols] = (oc * gcv_ref[:, cols]).astype(BF16)
        part = jnp.dot(ocg_ref[:, cols], w_ref[ATTN_WIDTH + c0:ATTN_WIDTH + c0 + _CONV_COLS, :],
                       preferred_element_type=F32)
        if c == 0:
            hc_ref[...] = part
        else:
            hc_ref[...] += part
    inv = jnp.concatenate(
        [lax.rsqrt(jnp.sum(s, axis=1, keepdims=True) * (1.0 / CONV_WIDTH) + EPS) for s in ssq],
        axis=0)
    h_ref[...] += inv * hc_ref[...]


def _out_proj(attn_n, pc, pc_meta, x, conv_w, g_conv_out, w_bf16, *, tm):
    seq = x.shape[0]
    hb = tm // _HALO
    seg = lambda s: (lambda i: (i, s))
    halo = lambda s: (lambda i: (jnp.maximum(i * hb - 1, 0), s))
    meta = lambda s: (lambda i: (0, s))
    const = lambda i: (0, 0)
    return pl.pallas_call(
        _out_proj_kernel,
        grid=(seq // tm,),
        in_specs=[
            pl.BlockSpec((tm, ATTN_WIDTH), lambda i: (i, 0)),
            pl.BlockSpec((tm, CONV_WIDTH), seg(_CSEG_GB)),
            pl.BlockSpec((tm, CONV_WIDTH), seg(_CSEG_GC)),
            pl.BlockSpec((tm, CONV_WIDTH), seg(_CSEG_U)),
            pl.BlockSpec((_HALO, CONV_WIDTH), halo(_CSEG_GC)),
            pl.BlockSpec((_HALO, CONV_WIDTH), halo(_CSEG_U)),
            pl.BlockSpec((N_META, CONV_WIDTH), meta(_CSEG_GC)),
            pl.BlockSpec((N_META, CONV_WIDTH), meta(_CSEG_U)),
            pl.BlockSpec((tm, D_MODEL), lambda i: (i, 0)),
            pl.BlockSpec((CONV_K, CONV_WIDTH), const),
            pl.BlockSpec((1, CONV_WIDTH), const),
            pl.BlockSpec((D_MODEL, D_MODEL), const),
        ],
        out_specs=pl.BlockSpec((tm, D_MODEL), lambda i: (i, 0)),
        out_shape=jax.ShapeDtypeStruct((seq, D_MODEL), F32),
        scratch_shapes=[pltpu.VMEM((tm, CONV_WIDTH), BF16), pltpu.VMEM((tm, D_MODEL), F32)],
        compiler_params=pltpu.CompilerParams(
            dimension_semantics=("arbitrary",),
            vmem_limit_bytes=56 * _MIB),
        name="out_proj",
    )(attn_n, pc, pc, pc, pc, pc, pc_meta, pc_meta, x, conv_w, g_conv_out, w_bf16)


def _ffn_kernel(h_ref, g_ref, wg_ref, wu_ref, wd_ref, o_ref, n_ref):
    f = pl.program_id(1)
    tm = h_ref.shape[0]

    def mlp(n):
        gate = jnp.dot(n, wg_ref[0], preferred_element_type=F32)
        up = jnp.dot(n, wu_ref[0], preferred_element_type=F32)
        act = (gate * (1.0 / (1.0 + jnp.exp(-gate))) * up).astype(BF16)
        return jnp.dot(act, wd_ref[...], preferred_element_type=F32)

    @pl.when(f == 0)
    def _():
        n_split = 2 if tm % (2 * _SPLIT_ALIGN) == 0 else 1
        for r in range(n_split):
            rows = slice(r * tm // n_split, (r + 1) * tm // n_split)
            h = h_ref[rows]
            n = _rms(h, g_ref[...]).astype(BF16)
            n_ref[rows] = n
            o_ref[rows] = h + mlp(n)

    @pl.when(f > 0)
    def _():
        o_ref[...] += mlp(n_ref[...])


def _ffn(h1, g_ffn, wg, wu, wd, *, tm):
    seq = h1.shape[0]
    n_f, _, tf = wg.shape
    assert wd.shape[0] == n_f * tf
    return pl.pallas_call(
        _ffn_kernel,
        grid=(seq // tm, n_f),
        in_specs=[
            pl.BlockSpec((tm, D_MODEL),
                         lambda i, f: (jnp.minimum(i + jnp.minimum(f, 1), seq // tm - 1), 0)),
            pl.BlockSpec((1, D_MODEL), lambda i, f: (0, 0)),
            pl.BlockSpec((1, D_MODEL, tf), lambda i, f: (f, 0, 0)),
            pl.BlockSpec((1, D_MODEL, tf), lambda i, f: (f, 0, 0)),
            pl.BlockSpec((tf, D_MODEL), lambda i, f: (f, 0)),
        ],
        out_specs=pl.BlockSpec((tm, D_MODEL), lambda i, f: (i, 0)),
        out_shape=jax.ShapeDtypeStruct((seq, D_MODEL), F32),
        scratch_shapes=[pltpu.VMEM((tm, D_MODEL), BF16)],
        compiler_params=pltpu.CompilerParams(
            dimension_semantics=("arbitrary", "arbitrary"),
            vmem_limit_bytes=60 * _MIB),
        name="ffn",
    )(h1, g_ffn, wg, wu, wd)


def kernel(x, meta_tokens, g_mix, w_in, g_q, g_k, conv_w, g_attn_out, g_conv_out,
           w_out, g_ffn, w_gate, w_up, w_down):
    batch, seq, _ = x.shape
    depth = w_in.shape[0]
    assert batch == 1 and depth == 1
    xs = x[0]
    tq = 256
    tf = 512

    qkv_meta_pad, pc_meta, w_in_b = _in_proj(meta_tokens, g_mix, w_in[0], g_q, g_k,
                                             tm=N_META, qkv_rows=tq)
    qkv, pc, w_out_b = _in_proj(xs, g_mix, w_in_b, g_q, g_k, side=((w_out[0], None),), tm=1024)

    attn_n, w_gate_b, w_up_b, w_down_b = _attention(
        qkv, qkv_meta_pad, g_attn_out,
        side=((w_gate[0], tf), (w_up[0], tf), (w_down[0], None)), tq=tq)
    h1 = _out_proj(attn_n, pc, pc_meta, xs, conv_w[0], g_conv_out, w_out_b, tm=512)
    out = _ffn(h1, g_ffn, w_gate_b, w_up_b, w_down_b, tm=1024)
    return out[None]
```

```python
import functools
import math

import jax
import jax.numpy as jnp
from jax import lax
from jax.experimental import pallas as pl
from jax.experimental.pallas import tpu as pltpu

D_MODEL = 2048
N_META = 16
ATTN_WIDTH = 1024
CONV_WIDTH = 1024
HEAD_DIM = 128
N_HEADS = ATTN_WIDTH // HEAD_DIM
CONV_K = 3
EPS = 1e-6

F32 = jnp.float32
BF16 = jnp.bfloat16

_N_SEG = 6
_N_ATTN_SEG = 3
_SEG_Q, _SEG_K, _SEG_V = 0, 1, 2
_CSEG_GB, _CSEG_GC, _CSEG_U = 0, 1, 2

_MIB = 1024 * 1024

_SKIP = 110.0


def _rms(x, g):
    return x * lax.rsqrt(jnp.mean(x * x, axis=-1, keepdims=True) + EPS) * g


_N_CHUNK = 4
_CHUNK = ATTN_WIDTH // _N_CHUNK
_HEADS_PER_CHUNK = _CHUNK // HEAD_DIM
_CAST_STEPS = 32
_SPLIT_ALIGN = 256


def _cast_specs(a, chunk, step_of):
    rows = a.shape[0] // _CAST_STEPS
    assert rows * _CAST_STEPS == a.shape[0] and rows % 16 == 0

    def step(*idx):
        return jnp.minimum(step_of(*idx), _CAST_STEPS - 1)
    in_spec = pl.BlockSpec((rows, a.shape[1]), lambda *idx: (step(*idx), 0))
    if chunk is None:
        return in_spec, in_spec, jax.ShapeDtypeStruct(a.shape, BF16)
    n = a.shape[1] // chunk
    assert n * chunk == a.shape[1]
    out_spec = pl.BlockSpec((n, rows, chunk), lambda *idx: (0, step(*idx), 0))
    return in_spec, out_spec, jax.ShapeDtypeStruct((n, a.shape[0], chunk), BF16)


def _cast_side(side_in, side_out):
    for src, dst in zip(side_in, side_out):
        if len(dst.shape) == 2:
            dst[...] = src[...].astype(BF16)
        else:
            n, _, chunk = dst.shape
            for f in range(n):
                dst[f] = src[:, f * chunk:(f + 1) * chunk].astype(BF16)


def _in_proj_kernel(*refs, cast_w, n_side):
    x_ref, g_ref, w_ref, gq_ref, gk_ref = refs[:5]
    side_in = refs[5:5 + n_side]
    outs = refs[5 + n_side:]
    qkv_ref, pc_ref = outs[:2]
    outs = outs[2:]
    if cast_w:
        wb_ref, outs = outs[0], outs[1:]
    side_out, xn_ref = outs[:n_side], outs[n_side]
    j = pl.program_id(1)
    tm = x_ref.shape[0]
    all_rows = slice(0, tm)

    def chunks(epilogue, rows=all_rows, xn=None, first=True):
        if first:
            _cast_side(side_in, side_out)
        if xn is None:
            xn = xn_ref[rows]
        for c in range(_N_CHUNK):
            cols = slice(c * _CHUNK, (c + 1) * _CHUNK)
            if cast_w:
                if first:
                    wb_ref[0, :, cols] = w_ref[:, cols].astype(BF16)
                w = wb_ref[0, :, cols]
            else:
                w = w_ref[0, :, cols]
            epilogue(c, rows, jnp.dot(xn, w, preferred_element_type=F32))

    def heads_out(fn):
        def epilogue(c, rows, acc):
            for h in range(_HEADS_PER_CHUNK):
                hd = c * _HEADS_PER_CHUNK + h
                a = acc[:, h * HEAD_DIM:(h + 1) * HEAD_DIM]
                if qkv_ref.shape[1] != tm:
                    qkv_ref[hd] = jnp.zeros(qkv_ref.shape[1:], BF16)
                qkv_ref[hd, rows] = fn(a).astype(BF16)
        return epilogue

    @pl.when(j == _SEG_Q)
    def _():
        gain = gq_ref[...] * (1.0 / math.sqrt(HEAD_DIM))
        n_split = 2 if tm % (2 * _SPLIT_ALIGN) == 0 else 1
        for r in range(n_split):
            rows = slice(r * tm // n_split, (r + 1) * tm // n_split)
            xn = _rms(x_ref[rows], g_ref[...]).astype(BF16)
            xn_ref[rows] = xn
            chunks(heads_out(lambda a: _rms(a, gain)), rows, xn, first=(r == 0))

    @pl.when(j == _SEG_K)
    def _():
        gain = gk_ref[...]
        chunks(heads_out(lambda a: _rms(a, gain)))

    @pl.when(j == _SEG_V)
    def _():
        chunks(heads_out(lambda a: a))

    @pl.when(j >= _N_ATTN_SEG)
    def _():
        def epilogue(c, rows, acc):
            pc_ref[rows, c * _CHUNK:(c + 1) * _CHUNK] = acc.astype(BF16)
        chunks(epilogue)


def _in_proj(x, g_mix, w, g_q, g_k, side=(), *, tm, qkv_rows=None):
    m = x.shape[0]
    tn = ATTN_WIDTH
    n_i = m // tm
    cast_w = w.dtype != BF16
    if qkv_rows is None:
        qkv_rows = tm
    assert qkv_rows == tm or n_i == 1

    assert not side or n_i * _N_SEG >= _CAST_STEPS
    casts = [_cast_specs(a, chunk, lambda i, j: i * _N_SEG + j) for a, chunk in side]
    w_seg_spec = pl.BlockSpec((1, D_MODEL, tn), lambda i, j: (j, 0, 0))
    out_specs = [
        pl.BlockSpec((N_HEADS, qkv_rows, HEAD_DIM),
                     lambda i, j: (jnp.minimum(j, _N_ATTN_SEG - 1), i, 0)),
        pl.BlockSpec((tm, CONV_WIDTH),
                     lambda i, j: (i, jnp.maximum(j - _N_ATTN_SEG, 0))),
    ]
    out_shape = [
        jax.ShapeDtypeStruct((_N_ATTN_SEG * N_HEADS, n_i * qkv_rows, HEAD_DIM), BF16),
        jax.ShapeDtypeStruct((m, (_N_SEG - _N_ATTN_SEG) * CONV_WIDTH), BF16),
    ]
    if cast_w:
        assert n_i == 1
        out_specs.append(w_seg_spec)
        out_shape.append(jax.ShapeDtypeStruct((_N_SEG, D_MODEL, tn), BF16))
        w_spec = pl.BlockSpec((D_MODEL, tn), lambda i, j: (0, j))
    else:
        w_spec = w_seg_spec
    out_specs += [c[1] for c in casts]
    out_shape += [c[2] for c in casts]
    return pl.pallas_call(
        functools.partial(_in_proj_kernel, cast_w=cast_w, n_side=len(side)),
        grid=(n_i, _N_SEG),
        in_specs=[
            pl.BlockSpec((tm, D_MODEL), lambda i, j: (i, 0)),
            pl.BlockSpec((1, D_MODEL), lambda i, j: (0, 0)),
            w_spec,
            pl.BlockSpec((1, HEAD_DIM), lambda i, j: (0, 0)),
            pl.BlockSpec((1, HEAD_DIM), lambda i, j: (0, 0)),
        ] + [c[0] for c in casts],
        out_specs=out_specs,
        out_shape=out_shape,
        scratch_shapes=[pltpu.VMEM((tm, D_MODEL), BF16)],
        compiler_params=pltpu.CompilerParams(
            dimension_semantics=("arbitrary", "arbitrary"),
            vmem_limit_bytes=56 * _MIB),
        name="in_proj",
    )(x, g_mix, w, g_q, g_k, *[a for a, _ in side])


_MASKED = -1e30


def _attn_blocks(qs, kbs, vbs, tri, masks, accs, carries):
    n, nb = len(qs), len(masks)
    ss = [[lax.dot_general(qs[c], kbs[c][b], (((1,), (1,)), ((), ())),
                           preferred_element_type=F32) for b in range(nb)] for c in range(n)]
    parts = []
    for c in range(n):
        row = []
        for b in range(nb):
            s = ss[c][b] if masks[b] is None else jnp.where(masks[b], ss[c][b], _MASKED)
            sp = jnp.maximum(s, 0.0) + jnp.log(1.0 + jnp.exp(-jnp.abs(s)))
            lb = s - sp
            hi = sp.astype(BF16)
            lo = (sp - hi.astype(F32)).astype(BF16)
            cum = (jnp.dot(hi, tri, preferred_element_type=F32)
                   + jnp.dot(lo, tri, preferred_element_type=F32))
            row.append((sp, lb, cum))
        parts.append(row)
    out = []
    for c in range(n):
        acc, carry = accs[c], carries[c]
        for b, (sp, lb, cum) in enumerate(parts[c]):
            w = jnp.exp(lb - cum - carry)
            acc = acc + jnp.dot(w.astype(BF16), vbs[c][b], preferred_element_type=F32)
            carry = carry + jnp.sum(sp, axis=1, keepdims=True)
        out.append((acc, carry))
    return out


def _attn_kernel(*refs, tq, n_side):
    q_ref, k_ref, v_ref, km_ref, vm_ref, g_ref = refs[:6]
    side_in = refs[6:6 + n_side]
    o_ref = refs[6 + n_side]
    side_out = refs[7 + n_side:7 + 2 * n_side]
    acc_ref, car_ref = refs[7 + 2 * n_side:]
    heads = range(N_HEADS)
    _cast_side(side_in, side_out)

    i = pl.program_id(0)
    row = lax.broadcasted_iota(jnp.int32, (tq, tq), 0)
    col = lax.broadcasted_iota(jnp.int32, (tq, tq), 1)
    tri = (row > col).astype(BF16)
    diag_mask = col < row
    meta_mask = col < N_META
    base = pl.multiple_of(i * tq, tq)

    def sweep(blocks, first):
        if first:
            init = [(jnp.zeros((tq, HEAD_DIM), F32), jnp.zeros((tq, 1), F32))] * N_HEADS
        else:
            init = [(acc_ref[h], car_ref[h]) for h in heads]

        def block(ref, meta_ref, h, off):
            return meta_ref[h] if off is None else ref[h, pl.ds(off, tq), :]
        kbs = [[block(k_ref, km_ref, h, off) for off, _ in blocks] for h in heads]
        vbs = [[block(v_ref, vm_ref, h, off) for off, _ in blocks] for h in heads]
        res = _attn_blocks([q_ref[h] for h in heads], kbs, vbs, tri, [m for _, m in blocks],
                           [a for a, _ in init], [c for _, c in init])
        cmin = None
        for h, (acc, car) in enumerate(res):
            acc_ref[h] = acc
            car_ref[h] = car
            m = jnp.min(car)
            cmin = m if cmin is None else jnp.minimum(cmin, m)
        return cmin

    cmin = sweep([(base, diag_mask)], True)

    def cond(st):
        return jnp.logical_and(st[0] < i, st[1] < _SKIP)

    def body(st):
        off = pl.multiple_of((i - 1 - st[0]) * tq, tq)
        return st[0] + 1, sweep([(off, None)], False)

    _, cmin = lax.while_loop(cond, body, (jnp.int32(0), cmin))

    @pl.when(cmin < _SKIP)
    def _():
        sweep([(None, meta_mask)], False)

    ssq = jnp.zeros((tq, HEAD_DIM), F32)
    for h in heads:
        o = acc_ref[h]
        ssq = ssq + o * o
    inv = lax.rsqrt(jnp.sum(ssq, axis=1, keepdims=True) * (1.0 / ATTN_WIDTH) + EPS)
    for h in heads:
        sl = slice(h * HEAD_DIM, (h + 1) * HEAD_DIM)
        o_ref[:, sl] = (acc_ref[h] * inv * g_ref[:, sl]).astype(BF16)


def _attention(qkv, qkv_meta_pad, g_attn_out, side=(), *, tq):
    seq = qkv.shape[1]
    mpad = qkv_meta_pad.shape[1]
    assert mpad == tq
    assert not side or seq // tq >= _CAST_STEPS
    resident = dict(pipeline_mode=pl.Buffered(1))
    casts = [_cast_specs(a, chunk, lambda i: i) for a, chunk in side]
    return pl.pallas_call(
        functools.partial(_attn_kernel, tq=tq, n_side=len(side)),
        grid=(seq // tq,),
        in_specs=[
            pl.BlockSpec((N_HEADS, tq, HEAD_DIM), lambda i: (_SEG_Q, i, 0)),
            pl.BlockSpec((N_HEADS, seq, HEAD_DIM), lambda i: (_SEG_K, 0, 0), **resident),
            pl.BlockSpec((N_HEADS, seq, HEAD_DIM), lambda i: (_SEG_V, 0, 0), **resident),
            pl.BlockSpec((N_HEADS, mpad, HEAD_DIM), lambda i: (_SEG_K, 0, 0), **resident),
            pl.BlockSpec((N_HEADS, mpad, HEAD_DIM), lambda i: (_SEG_V, 0, 0), **resident),
            pl.BlockSpec((1, ATTN_WIDTH), lambda i: (0, 0)),
        ] + [c[0] for c in casts],
        out_specs=[pl.BlockSpec((tq, ATTN_WIDTH), lambda i: (i, 0))] + [c[1] for c in casts],
        out_shape=[jax.ShapeDtypeStruct((seq, ATTN_WIDTH), BF16)] + [c[2] for c in casts],
        scratch_shapes=[
            pltpu.VMEM((N_HEADS, tq, HEAD_DIM), F32),
            pltpu.VMEM((N_HEADS, tq, 1), F32),
        ],
        compiler_params=pltpu.CompilerParams(
            dimension_semantics=("arbitrary",),
            vmem_limit_bytes=60 * _MIB),
        name="attn",
    )(qkv, qkv, qkv, qkv_meta_pad, qkv_meta_pad, g_attn_out, *[a for a, _ in side])


_HALO = 16
_CONV_ROWS = 64
_CONV_COLS = 256


def _out_proj_kernel(an_ref, gb_ref, gc_ref, u_ref, gch_ref, uh_ref, gcm_ref, um_ref,
                     x_ref, cw_ref, gcv_ref, w_ref, h_ref, ocg_ref, hc_ref):
    i = pl.program_id(0)
    tm = an_ref.shape[0]
    n_rows = tm // _CONV_ROWS
    n_cols = CONV_WIDTH // _CONV_COLS
    nw = D_MODEL // n_cols
    an = an_ref[...]
    ssq = [jnp.zeros((_CONV_ROWS, 128), F32)] * n_rows
    for c in range(n_cols):
        c0 = c * _CONV_COLS
        cols = slice(c0, c0 + _CONV_COLS)
        ocols = slice(c * nw, (c + 1) * nw)
        h_ref[:, ocols] = x_ref[:, ocols] + jnp.dot(an, w_ref[:ATTN_WIDTH, ocols],
                                                    preferred_element_type=F32)
        for r in range(n_rows):
            r0 = r * _CONV_ROWS
            if r0 == 0:
                h_seq = gch_ref[:, cols].astype(F32) * uh_ref[:, cols].astype(F32)
                h_meta = gcm_ref[:, cols].astype(F32) * um_ref[:, cols].astype(F32)
                halo = jnp.where(i == 0, h_meta, h_seq)
                cu = gc_ref[:_CONV_ROWS, cols].astype(F32) * u_ref[:_CONV_ROWS, cols].astype(F32)
                ext = jnp.concatenate([halo, cu], axis=0)
            else:
                rows = slice(r0 - _HALO, r0 + _CONV_ROWS)
                ext = gc_ref[rows, cols].astype(F32) * u_ref[rows, cols].astype(F32)
            cw = cw_ref[:, cols]
            y = (cw[0:1] * pltpu.roll(ext, 2, axis=0)[_HALO:]
                 + cw[1:2] * pltpu.roll(ext, 1, axis=0)[_HALO:]
                 + cw[2:3] * ext[_HALO:])
            oc = gb_ref[r0:r0 + _CONV_ROWS, cols].astype(F32) * y
            sq = oc * oc
            for l0 in range(0, _CONV_COLS, 128):
                ssq[r] = ssq[r] + sq[:, l0:l0 + 128]
            ocg_ref[r0:r0 + _CONV_ROWS, cols] = (oc * gcv_ref[:, cols]).astype(BF16)
        part = jnp.dot(ocg_ref[:, cols], w_ref[ATTN_WIDTH + c0:ATTN_WIDTH + c0 + _CONV_COLS, :],
                       preferred_element_type=F32)
        if c == 0:
            hc_ref[...] = part
        else:
            hc_ref[...] += part
    inv = jnp.concatenate(
        [lax.rsqrt(jnp.sum(s, axis=1, keepdims=True) * (1.0 / CONV_WIDTH) + EPS) for s in ssq],
        axis=0)
    h_ref[...] += inv * hc_ref[...]


def _out_proj(attn_n, pc, pc_meta, x, conv_w, g_conv_out, w_bf16, *, tm):
    seq = x.shape[0]
    hb = tm // _HALO
    seg = lambda s: (lambda i: (i, s))
    halo = lambda s: (lambda i: (jnp.maximum(i * hb - 1, 0), s))
    meta = lambda s: (lambda i: (0, s))
    const = lambda i: (0, 0)
    return pl.pallas_call(
        _out_proj_kernel,
        grid=(seq // tm,),
        in_specs=[
            pl.BlockSpec((tm, ATTN_WIDTH), lambda i: (i, 0)),
            pl.BlockSpec((tm, CONV_WIDTH), seg(_CSEG_GB)),
            pl.BlockSpec((tm, CONV_WIDTH), seg(_CSEG_GC)),
            pl.BlockSpec((tm, CONV_WIDTH), seg(_CSEG_U)),
            pl.BlockSpec((_HALO, CONV_WIDTH), halo(_CSEG_GC)),
            pl.BlockSpec((_HALO, CONV_WIDTH), halo(_CSEG_U)),
            pl.BlockSpec((N_META, CONV_WIDTH), meta(_CSEG_GC)),
            pl.BlockSpec((N_META, CONV_WIDTH), meta(_CSEG_U)),
            pl.BlockSpec((tm, D_MODEL), lambda i: (i, 0)),
            pl.BlockSpec((CONV_K, CONV_WIDTH), const),
            pl.BlockSpec((1, CONV_WIDTH), const),
            pl.BlockSpec((D_MODEL, D_MODEL), const),
        ],
        out_specs=pl.BlockSpec((tm, D_MODEL), lambda i: (i, 0)),
        out_shape=jax.ShapeDtypeStruct((seq, D_MODEL), F32),
        scratch_shapes=[pltpu.VMEM((tm, CONV_WIDTH), BF16), pltpu.VMEM((tm, D_MODEL), F32)],
        compiler_params=pltpu.CompilerParams(
            dimension_semantics=("arbitrary",),
            vmem_limit_bytes=56 * _MIB),
        name="out_proj",
    )(attn_n, pc, pc, pc, pc, pc, pc_meta, pc_meta, x, conv_w, g_conv_out, w_bf16)


def _ffn_kernel(h_ref, g_ref, wg_ref, wu_ref, wd_ref, o_ref, n_ref):
    f = pl.program_id(1)
    tm = h_ref.shape[0]

    def mlp(n):
        gate = jnp.dot(n, wg_ref[0], preferred_element_type=F32)
        up = jnp.dot(n, wu_ref[0], preferred_element_type=F32)
        act = (gate * (1.0 / (1.0 + jnp.exp(-gate))) * up).astype(BF16)
        return jnp.dot(act, wd_ref[...], preferred_element_type=F32)

    @pl.when(f == 0)
    def _():
        n_split = 2 if tm % (2 * _SPLIT_ALIGN) == 0 else 1
        for r in range(n_split):
            rows = slice(r * tm // n_split, (r + 1) * tm // n_split)
            h = h_ref[rows]
            n = _rms(h, g_ref[...]).astype(BF16)
            n_ref[rows] = n
            o_ref[rows] = h + mlp(n)

    @pl.when(f > 0)
    def _():
        o_ref[...] += mlp(n_ref[...])


def _ffn(h1, g_ffn, wg, wu, wd, *, tm):
    seq = h1.shape[0]
    n_f, _, tf = wg.shape
    assert wd.shape[0] == n_f * tf
    return pl.pallas_call(
        _ffn_kernel,
        grid=(seq // tm, n_f),
        in_specs=[
            pl.BlockSpec((tm, D_MODEL),
                         lambda i, f: (jnp.minimum(i + jnp.minimum(f // (n_f // 2), 1),
                                                   seq // tm - 1), 0)),
            pl.BlockSpec((1, D_MODEL), lambda i, f: (0, 0)),
            pl.BlockSpec((1, D_MODEL, tf), lambda i, f: (f, 0, 0)),
            pl.BlockSpec((1, D_MODEL, tf), lambda i, f: (f, 0, 0)),
            pl.BlockSpec((tf, D_MODEL), lambda i, f: (f, 0)),
        ],
        out_specs=pl.BlockSpec((tm, D_MODEL), lambda i, f: (i, 0)),
        out_shape=jax.ShapeDtypeStruct((seq, D_MODEL), F32),
        scratch_shapes=[pltpu.VMEM((tm, D_MODEL), BF16)],
        compiler_params=pltpu.CompilerParams(
            dimension_semantics=("arbitrary", "arbitrary"),
            vmem_limit_bytes=60 * _MIB),
        name="ffn",
    )(h1, g_ffn, wg, wu, wd)


def kernel(x, meta_tokens, g_mix, w_in, g_q, g_k, conv_w, g_attn_out, g_conv_out,
           w_out, g_ffn, w_gate, w_up, w_down):
    batch, seq, _ = x.shape
    depth = w_in.shape[0]
    assert batch == 1 and depth == 1
    xs = x[0]
    tq = 256
    tf = 512

    qkv_meta_pad, pc_meta, w_in_b = _in_proj(meta_tokens, g_mix, w_in[0], g_q, g_k,
                                             tm=N_META, qkv_rows=tq)
    qkv, pc, w_out_b = _in_proj(xs, g_mix, w_in_b, g_q, g_k, side=((w_out[0], None),), tm=1024)

    attn_n, w_gate_b, w_up_b, w_down_b = _attention(
        qkv, qkv_meta_pad, g_attn_out,
        side=((w_gate[0], tf), (w_up[0], tf), (w_down[0], None)), tq=tq)
    h1 = _out_proj(attn_n, pc, pc_meta, xs, conv_w[0], g_conv_out, w_out_b, tm=512)
    out = _ffn(h1, g_ffn, w_gate_b, w_up_b, w_down_b, tm=1024)
    return out[None]
```
